```python
import jax, jax.numpy as jnp
from jax import lax
import numpy as np

D_MODEL = 2048
BATCH = 4
SEQ = 2048
DEPTH = 1

HEAD_DIM = 128
N_HEADS_MOBA = 8
N_HEADS_SB = 8
D_MOBA = N_HEADS_MOBA * HEAD_DIM
D_SB = N_HEADS_SB * HEAD_DIM
D_MIX = D_MOBA + D_SB
D_QKV = 3 * D_MIX
D_FF = 4 * D_MODEL
MOBA_BLOCK = 256
MOBA_TOPK = 3
MOBA_Q_CHUNK = 16
SB_Q_BLOCK = 128
ROPE_THETA = 500000.0
ROPE_DIMS = HEAD_DIM // 4
EPS = 1e-6
NEG = -1e30

kernel_name = "hymba_moba_stickbreaking_sqrelu_block"


def rmsnorm(x, g):
    xf = x.astype(jnp.float32)
    y = xf * lax.rsqrt(jnp.mean(xf * xf, axis=-1, keepdims=True) + EPS)
    return (y * g.astype(jnp.float32)).astype(x.dtype)


def partial_rope(t, pos):
    half = ROPE_DIMS // 2
    inv_freq = ROPE_THETA ** (-jnp.arange(half, dtype=jnp.float32) / half)
    ang = pos.astype(jnp.float32)[:, None] * inv_freq[None, :]
    cos, sin = jnp.cos(ang), jnp.sin(ang)
    tr = t[..., :ROPE_DIMS].astype(jnp.float32)
    t1, t2 = tr[..., :half], tr[..., half:]
    rot = jnp.concatenate([t1 * cos - t2 * sin, t1 * sin + t2 * cos], axis=-1)
    return jnp.concatenate([rot.astype(t.dtype), t[..., ROPE_DIMS:]], axis=-1)


def moba_attention(q, k, v):
    B, H, S, Dh = q.shape
    n_blk = -(-S // MOBA_BLOCK)
    pad = n_blk * MOBA_BLOCK - S
    kb = jnp.pad(k, ((0, 0), (0, 0), (0, pad), (0, 0))).reshape(B, H, n_blk, MOBA_BLOCK, Dh)
    vb = jnp.pad(v, ((0, 0), (0, 0), (0, pad), (0, 0))).reshape(B, H, n_blk, MOBA_BLOCK, Dh)
    topk = min(MOBA_TOPK, n_blk)
    scale = Dh ** -0.5

    k_mean = jnp.mean(kb.astype(jnp.float32), axis=3)
    gate = jnp.einsum('bhsd,bhnd->bhsn', q.astype(jnp.float32), k_mean)
    q_blk = jnp.arange(S) // MOBA_BLOCK
    past = jnp.arange(n_blk)[None, :] < q_blk[:, None]
    gate = jnp.where(past, gate, NEG)
    _, sel = lax.top_k(gate, topk)

    n_chunk = S // MOBA_Q_CHUNK

    def to_chunks(a):
        return jnp.moveaxis(a.reshape(B, H, n_chunk, MOBA_Q_CHUNK, *a.shape[3:]), 2, 0)

    b_idx = jnp.arange(B)[:, None, None, None]
    h_idx = jnp.arange(H)[None, :, None, None]
    key_off = jnp.arange(MOBA_BLOCK)

    def chunk_fn(args):
        c, qc, selc = args
        q_pos = c * MOBA_Q_CHUNK + jnp.arange(MOBA_Q_CHUNK)
        blk = (c * MOBA_Q_CHUNK) // MOBA_BLOCK
        kg = kb[b_idx, h_idx, selc]
        vg = vb[b_idx, h_idx, selc]
        s_past = jnp.einsum('bhcd,bhcrkd->bhcrk', qc, kg,
                            preferred_element_type=jnp.float32) * scale
        slot_valid = jnp.arange(topk) < blk
        s_past = jnp.where(slot_valid[:, None], s_past, NEG)
        s_past = s_past.reshape(B, H, MOBA_Q_CHUNK, topk * MOBA_BLOCK)
        k_own = lax.dynamic_index_in_dim(kb, blk, axis=2, keepdims=False)
        v_own = lax.dynamic_index_in_dim(vb, blk, axis=2, keepdims=False)
        s_own = jnp.einsum('bhcd,bhkd->bhck', qc, k_own,
                           preferred_element_type=jnp.float32) * scale
        own_pos = blk * MOBA_BLOCK + key_off
        s_own = jnp.where(own_pos[None, :] <= q_pos[:, None], s_own, NEG)
        p = jax.nn.softmax(jnp.concatenate([s_past, s_own], axis=-1), axis=-1)
        p_past = p[..., :topk * MOBA_BLOCK].astype(v.dtype)
        p_own = p[..., topk * MOBA_BLOCK:].astype(v.dtype)
        o = jnp.einsum('bhck,bhckd->bhcd', p_past,
                       vg.reshape(B, H, MOBA_Q_CHUNK, topk * MOBA_BLOCK, Dh),
                       preferred_element_type=jnp.float32)
        o = o + jnp.einsum('bhck,bhkd->bhcd', p_own, v_own, preferred_element_type=jnp.float32)
        return o.astype(q.dtype)

    out = lax.map(chunk_fn, (jnp.arange(n_chunk), to_chunks(q), to_chunks(sel)))
    return jnp.moveaxis(out, 0, 2).reshape(B, H, S, Dh)


def stick_breaking_attention(q, k, v):
    B, H, S, Dh = q.shape
    n_qblk = S // SB_Q_BLOCK
    scale = Dh ** -0.5
    key_pos = jnp.arange(S)
    q_blocks = jnp.moveaxis(q.reshape(B, H, n_qblk, SB_Q_BLOCK, Dh), 2, 0)

    def block_fn(args):
        i, qb = args
        q_pos = i * SB_Q_BLOCK + jnp.arange(SB_Q_BLOCK)
        z = jnp.einsum('bhqd,bhkd->bhqk', qb, k, preferred_element_type=jnp.float32) * scale
        causal = key_pos[None, :] < q_pos[:, None]
        log_beta = jax.nn.log_sigmoid(z)
        log_1m_beta = jnp.where(causal, jax.nn.log_sigmoid(-z), 0.0)
        log_stay = lax.cumsum(log_1m_beta, axis=3, reverse=True) - log_1m_beta
        a = jnp.where(causal, jnp.exp(log_beta + log_stay), 0.0)
        o = jnp.einsum('bhqk,bhkd->bhqd', a.astype(v.dtype), v, preferred_element_type=jnp.float32)
        return o.astype(q.dtype)

    out = lax.map(block_fn, (jnp.arange(n_qblk), q_blocks))
    return jnp.moveaxis(out, 0, 2).reshape(B, H, S, Dh)


def setup_inputs(seed: int = 0) -> dict:
    key = jax.random.key(seed)
    ks = jax.random.split(key, 12)
    f32 = jnp.float32
    x = jax.random.normal(ks[0], (BATCH, SEQ, D_MODEL), f32)
    mix_norm_g = 1.0 + 0.02 * jax.random.normal(ks[1], (DEPTH, D_MODEL), f32)
    w_in = jax.random.normal(ks[2], (DEPTH, D_MODEL, D_QKV), f32) * D_MODEL ** -0.5
    moba_out_g = 1.0 + 0.02 * jax.random.normal(ks[3], (DEPTH, D_MOBA), f32)
    sb_out_g = 1.0 + 0.02 * jax.random.normal(ks[4], (DEPTH, D_SB), f32)
    w_out = jax.random.normal(ks[5], (DEPTH, D_MIX, D_MODEL), f32) * D_MIX ** -0.5
    mlp_norm_g = 1.0 + 0.02 * jax.random.normal(ks[6], (DEPTH, D_MODEL), f32)
    w_up = jax.random.normal(ks[7], (DEPTH, D_MODEL, D_FF), f32) * D_MODEL ** -0.5
    w_down = jax.random.normal(ks[8], (DEPTH, D_FF, D_MODEL), f32) * D_FF ** -0.5
    final_norm_g = 1.0 + 0.02 * jax.random.normal(ks[9], (D_MODEL,), f32)
    return {"x": x, "mix_norm_g": mix_norm_g, "w_in": w_in, "moba_out_g": moba_out_g,
            "sb_out_g": sb_out_g, "w_out": w_out, "mlp_norm_g": mlp_norm_g,
            "w_up": w_up, "w_down": w_down, "final_norm_g": final_norm_g}


def reference(x, mix_norm_g, w_in, moba_out_g, sb_out_g, w_out, mlp_norm_g, w_up, w_down,
              final_norm_g):
    B, S, _ = x.shape
    pos = jnp.arange(S)

    def heads(t, n_heads):
        return t.reshape(B, S, n_heads, HEAD_DIM).transpose(0, 2, 1, 3)

    def merge(t):
        return t.transpose(0, 2, 1, 3).reshape(B, S, -1)

    for l in range(DEPTH):
        h = rmsnorm(x, mix_norm_g[l])
        qkv = jnp.einsum('bsd,de->bse', h, w_in[l])
        q_a, k_a, v_a, q_b, k_b, v_b = jnp.split(
            qkv, [D_MOBA, 2 * D_MOBA, 3 * D_MOBA, 3 * D_MOBA + D_SB, 3 * D_MOBA + 2 * D_SB], axis=-1)
        q_a = partial_rope(heads(q_a, N_HEADS_MOBA), pos)
        k_a = partial_rope(heads(k_a, N_HEADS_MOBA), pos)
        o_a = moba_attention(q_a, k_a, heads(v_a, N_HEADS_MOBA))
        o_b = stick_breaking_attention(heads(q_b, N_HEADS_SB), heads(k_b, N_HEADS_SB),
                                       heads(v_b, N_HEADS_SB))
        o = jnp.concatenate([rmsnorm(merge(o_a), moba_out_g[l]),
                             rmsnorm(merge(o_b), sb_out_g[l])], axis=-1)
        x = x + jnp.einsum('bse,ed->bsd', o, w_out[l])
        h = rmsnorm(x, mlp_norm_g[l])
        u = jnp.einsum('bsd,df->bsf', h, w_up[l])
        x = x + jnp.einsum('bsf,fd->bsd', jnp.square(jax.nn.relu(u)), w_down[l])
    return rmsnorm(x, final_norm_g)
```

```python
import functools

import jax
import jax.numpy as jnp
import numpy as np
from jax import lax
from jax.experimental import pallas as pl
from jax.experimental.pallas import tpu as pltpu

HEAD_DIM = 128
N_HEADS_MOBA = 8
N_HEADS_SB = 8
MOBA_BLOCK = 256
MOBA_TOPK = 3
ROPE_THETA = 500000.0
ROPE_DIMS = HEAD_DIM // 4
EPS = 1e-6
NEG = -1e30

F32 = jnp.float32
BF16 = jnp.bfloat16

_NT = (((1,), (1,)), ((), ()))

VMEM_LIMIT_BYTES = 56 * 1024 * 1024


def _rms_scale(x):
    return lax.rsqrt(jnp.mean(x * x, axis=-1, keepdims=True) + EPS)


def _qkv_kernel(x_ref, g_ref, w_ref, cos_ref, sin_lo_ref, sin_hi_ref, o_ref, h_ref, *, scale):
    j = pl.program_id(1)

    @pl.when(j == 0)
    def _():
        x = x_ref[...]
        h_ref[...] = ((x * _rms_scale(x)) * g_ref[...]).astype(BF16)

    def project():
        return jnp.dot(h_ref[...], w_ref[...], preferred_element_type=F32)

    def rope(y, post_scale):
        n_heads = y.shape[1] // HEAD_DIM
        half = ROPE_DIMS // 2
        for hd in range(n_heads):
            t = y[:, hd * HEAD_DIM:(hd + 1) * HEAD_DIM]
            from_lo = pltpu.roll(t, half, 1)
            from_hi = pltpu.roll(t, HEAD_DIM - half, 1)
            r = t * cos_ref[...] + from_lo * sin_lo_ref[...] + from_hi * sin_hi_ref[...]
            if post_scale is not None:
                r = r * post_scale
            o_ref[:, hd * HEAD_DIM:(hd + 1) * HEAD_DIM] = r.astype(o_ref.dtype)

    @pl.when(j == 0)
    def _():
        rope(project(), scale)

    @pl.when(j == 1)
    def _():
        rope(project(), None)

    @pl.when(j == 3)
    def _():
        o_ref[...] = (project() * scale).astype(o_ref.dtype)

    @pl.when((j == 2) | (j >= 4))
    def _():
        o_ref[...] = project().astype(o_ref.dtype)


def _qkv_proj(x2, g, w_bf16, cos_t, sin_lo_t, sin_hi_t, *, seq, tm, tn):
    m, d = x2.shape
    n = w_bf16.shape[1]
    pos_blocks = seq // tm
    kern = functools.partial(_qkv_kernel, scale=HEAD_DIM ** -0.5)
    tab_spec = pl.BlockSpec((tm, HEAD_DIM), lambda i, j: (i % pos_blocks, 0))
    return pl.pallas_call(
        kern,
        out_shape=jax.ShapeDtypeStruct((m, n), BF16),
        grid=(m // tm, n // tn),
        in_specs=[
            pl.BlockSpec((tm, d), lambda i, j: (i, 0)),
            pl.BlockSpec((1, d), lambda i, j: (0, 0)),
            pl.BlockSpec((d, tn), lambda i, j: (0, j)),
            tab_spec, tab_spec, tab_spec,
        ],
        out_specs=pl.BlockSpec((tm, tn), lambda i, j: (i, j)),
        scratch_shapes=[pltpu.VMEM((tm, d), BF16)],
        compiler_params=pltpu.CompilerParams(
            dimension_semantics=("arbitrary", "arbitrary"),
            vmem_limit_bytes=VMEM_LIMIT_BYTES),
        name="qkv_proj",
    )(x2, g, w_bf16, cos_t, sin_lo_t, sin_hi_t)


def _moba_kernel(q_ref, k_ref, v_ref, o_ref, vt_ref, *, seq):
    blk = MOBA_BLOCK
    n_blk = seq // blk

    kf = k_ref[...].astype(F32)
    k_mean = jnp.concatenate(
        [jnp.mean(kf[n * blk:(n + 1) * blk, :], axis=0, keepdims=True) for n in range(n_blk)], axis=0)
    km_hi = k_mean.astype(BF16)
    km_lo = (k_mean - km_hi.astype(F32)).astype(BF16)
    q_all = q_ref[...]
    gate = (lax.dot_general(km_hi, q_all, _NT, preferred_element_type=F32)
            + lax.dot_general(km_lo, q_all, _NT, preferred_element_type=F32))

    blk_id = lax.broadcasted_iota(jnp.int32, (n_blk, seq), 0)
    q_blk = lax.broadcasted_iota(jnp.int32, (n_blk, seq), 1) // blk
    past = blk_id < q_blk
    gate = jnp.where(past, gate, NEG)
    rank = jnp.zeros((n_blk, seq), jnp.int32)
    for mth in range(n_blk):
        g_m = gate[mth:mth + 1, :]
        beats = (g_m > gate) | ((g_m == gate) & (mth < blk_id))
        rank = rank + beats.astype(jnp.int32)
    selected = past & (rank < MOBA_TOPK)
    bias = jnp.where(selected, 0.0, NEG).astype(F32)

    vt_ref[...] = v_ref[...].astype(F32).T.astype(BF16)

    key_i = lax.broadcasted_iota(jnp.int32, (blk, blk), 0)
    qry_i = lax.broadcasted_iota(jnp.int32, (blk, blk), 1)
    causal = key_i <= qry_i

    for i in range(n_blk):
        q_i = q_ref[i * blk:(i + 1) * blk, :]
        s = lax.dot_general(k_ref[0:(i + 1) * blk, :], q_i, _NT,
                            preferred_element_type=F32)
        parts = [s[j * blk:(j + 1) * blk, :] + bias[j:j + 1, i * blk:(i + 1) * blk]
                 for j in range(i)]
        parts.append(jnp.where(causal, s[i * blk:(i + 1) * blk, :], NEG))
        s = jnp.concatenate(parts, axis=0) if i else parts[0]
        m = jnp.max(s, axis=0, keepdims=True)
        p = jnp.exp(s - m)
        l = jnp.sum(p, axis=0, keepdims=True)
        acc = jnp.dot(vt_ref[:, 0:(i + 1) * blk], p.astype(BF16),
                      preferred_element_type=F32)
        o_ref[i * blk:(i + 1) * blk, :] = (acc / l).T.astype(o_ref.dtype)


def _moba_attn(qkv, *, batch, seq):
    kern = functools.partial(_moba_kernel, seq=seq)
    hq, hk, hv = 0, N_HEADS_MOBA, 2 * N_HEADS_MOBA
    blk = (seq, HEAD_DIM)
    return pl.pallas_call(
        kern,
        out_shape=jax.ShapeDtypeStruct((batch * seq, N_HEADS_MOBA * HEAD_DIM), F32),
        grid=(batch, N_HEADS_MOBA),
        in_specs=[
            pl.BlockSpec(blk, lambda b, h: (b, hq + h)),
            pl.BlockSpec(blk, lambda b, h: (b, hk + h)),
            pl.BlockSpec(blk, lambda b, h: (b, hv + h)),
        ],
        out_specs=pl.BlockSpec(blk, lambda b, h: (b, h)),
        scratch_shapes=[pltpu.VMEM((HEAD_DIM, seq), BF16)],
        compiler_params=pltpu.CompilerParams(
            dimension_semantics=("arbitrary", "arbitrary"),
            vmem_limit_bytes=VMEM_LIMIT_BYTES),
        name="moba_attn",
    )(qkv, qkv, qkv)


def _sb_kernel(q_ref, k_ref, v_ref, o_ref, vt_ref, *, seq, tile):
    n_tiles = seq // tile
    vt_ref[...] = v_ref[...].astype(F32).T.astype(BF16)

    key_i = lax.broadcasted_iota(jnp.int32, (tile, tile), 0)
    qry_i = lax.broadcasted_iota(jnp.int32, (tile, tile), 1)
    causal = key_i < qry_i
    later_keys = (qry_i > key_i).astype(BF16)

    for i in range(n_tiles):
        q_i = q_ref[i * tile:(i + 1) * tile, :]
        z = lax.dot_general(k_ref[0:(i + 1) * tile, :], q_i, _NT,
                            preferred_element_type=F32)
        carry = jnp.zeros((1, tile), F32)
        weights = [None] * (i + 1)
        for j in range(i, -1, -1):
            z_j = z[j * tile:(j + 1) * tile, :]
            log_beta = jnp.minimum(z_j, 0.0) - jnp.log1p(jnp.exp(-jnp.abs(z_j)))
            log_1m_beta = log_beta - z_j
            if j == i:
                log_1m_beta = jnp.where(causal, log_1m_beta, 0.0)
            l1m_bf = log_1m_beta.astype(BF16)
            stay = jnp.dot(later_keys, l1m_bf, preferred_element_type=F32)
            a = jnp.exp(log_beta + stay + carry)
            if j == i:
                a = jnp.where(causal, a, 0.0)
            weights[j] = a.astype(BF16)
            carry = carry + stay[0:1, :] + l1m_bf[0:1, :].astype(F32)
        a_all = jnp.concatenate(weights, axis=0) if i else weights[0]
        acc = jnp.dot(vt_ref[:, 0:(i + 1) * tile], a_all, preferred_element_type=F32)
        o_ref[i * tile:(i + 1) * tile, :] = acc.T.astype(o_ref.dtype)


def _sb_attn(qkv, *, batch, seq, tile):
    kern = functools.partial(_sb_kernel, seq=seq, tile=tile)
    base = 3 * N_HEADS_MOBA
    hq, hk, hv = base, base + N_HEADS_SB, base + 2 * N_HEADS_SB
    blk = (seq, HEAD_DIM)
    return pl.pallas_call(
        kern,
        out_shape=jax.ShapeDtypeStruct((batch * seq, N_HEADS_SB * HEAD_DIM), F32),
        grid=(batch, N_HEADS_SB),
        in_specs=[
            pl.BlockSpec(blk, lambda b, h: (b, hq + h)),
            pl.BlockSpec(blk, lambda b, h: (b, hk + h)),
            pl.BlockSpec(blk, lambda b, h: (b, hv + h)),
        ],
        out_specs=pl.BlockSpec(blk, lambda b, h: (b, h)),
        scratch_shapes=[pltpu.VMEM((HEAD_DIM, seq), BF16)],
        compiler_params=pltpu.CompilerParams(
            dimension_semantics=("arbitrary", "arbitrary"),
            vmem_limit_bytes=VMEM_LIMIT_BYTES),
        name="sb_attn",
    )(qkv, qkv, qkv)


def _out_proj_kernel(oa_ref, ob_ref, ga_ref, gb_ref, w_ref, x_ref, y_ref):
    d_a = oa_ref.shape[1]
    oa = oa_ref[...]
    ob = ob_ref[...]
    na = ((oa * _rms_scale(oa)) * ga_ref[...]).astype(BF16)
    nb = ((ob * _rms_scale(ob)) * gb_ref[...]).astype(BF16)
    y = (jnp.dot(na, w_ref[0:d_a, :], preferred_element_type=F32)
         + jnp.dot(nb, w_ref[d_a:, :], preferred_element_type=F32))
    y_ref[...] = x_ref[...] + y


def _out_proj(o_a, o_b, g_a, g_b, w_bf16, x2, *, tm):
    m, d_a = o_a.shape
    d_b = o_b.shape[1]
    d = w_bf16.shape[1]
    return pl.pallas_call(
        _out_proj_kernel,
        out_shape=jax.ShapeDtypeStruct((m, d), F32),
        grid=(m // tm,),
        in_specs=[
            pl.BlockSpec((tm, d_a), lambda i: (i, 0)),
            pl.BlockSpec((tm, d_b), lambda i: (i, 0)),
            pl.BlockSpec((1, d_a), lambda i: (0, 0)),
            pl.BlockSpec((1, d_b), lambda i: (0, 0)),
            pl.BlockSpec((d_a + d_b, d), lambda i: (0, 0)),
            pl.BlockSpec((tm, d), lambda i: (i, 0)),
        ],
        out_specs=pl.BlockSpec((tm, d), lambda i: (i, 0)),
        compiler_params=pltpu.CompilerParams(
            dimension_semantics=("arbitrary",),
            vmem_limit_bytes=VMEM_LIMIT_BYTES),
        name="out_proj",
    )(o_a, o_b, g_a, g_b, w_bf16, x2)


def _mlp_kernel(x_ref, g_ref, wu_ref, wd_ref, gf_ref, o_ref, h_ref, *, n_chunk):
    f = pl.program_id(1)
    n_f = pl.num_programs(1)

    @pl.when(f == 0)
    def _():
        x = x_ref[...]
        h_ref[...] = ((x * _rms_scale(x)) * g_ref[...]).astype(BF16)

    u = jnp.dot(h_ref[...], wu_ref[...], preferred_element_type=F32)
    r = jnp.maximum(u, 0.0)
    act = (r * r).astype(BF16)
    d = o_ref.shape[1]
    cw = d // n_chunk

    @pl.when(f == 0)
    def _():
        for c in range(n_chunk):
            o_ref[:, c * cw:(c + 1) * cw] = jnp.dot(
                act, wd_ref[:, c * cw:(c + 1) * cw], preferred_element_type=F32)

    @pl.when(f > 0)
    def _():
        for c in range(n_chunk):
            o_ref[:, c * cw:(c + 1) * cw] += jnp.dot(
                act, wd_ref[:, c * cw:(c + 1) * cw], preferred_element_type=F32)

    @pl.when(f == n_f - 1)
    def _():
        y = x_ref[...] + o_ref[...]
        o_ref[...] = (y * _rms_scale(y)) * gf_ref[...]


def _mlp(x1, g, wu_bf16, wd_bf16, g_final, *, tm, tf):
    m, d = x1.shape
    d_ff = wu_bf16.shape[1]
    kern = functools.partial(_mlp_kernel, n_chunk=4)
    return pl.pallas_call(
        kern,
        out_shape=jax.ShapeDtypeStruct((m, d), F32),
        grid=(m // tm, d_ff // tf),
        in_specs=[
            pl.BlockSpec((tm, d), lambda i, f: (i, 0)),
            pl.BlockSpec((1, d), lambda i, f: (0, 0)),
            pl.BlockSpec((d, tf), lambda i, f: (0, f)),
            pl.BlockSpec((tf, d), lambda i, f: (f, 0)),
            pl.BlockSpec((1, d), lambda i, f: (0, 0)),
        ],
        out_specs=pl.BlockSpec((tm, d), lambda i, f: (i, 0)),
        scratch_shapes=[pltpu.VMEM((tm, d), BF16)],
        compiler_params=pltpu.CompilerParams(
            dimension_semantics=("arbitrary", "arbitrary"),
            vmem_limit_bytes=VMEM_LIMIT_BYTES),
        name="mlp",
    )(x1, g, wu_bf16, wd_bf16, g_final)


def _rope_tables(seq):
    half = ROPE_DIMS // 2
    inv_freq = ROPE_THETA ** (-jnp.arange(half, dtype=F32) / half)
    ang = jnp.arange(seq, dtype=F32)[:, None] * inv_freq[None, :]
    cos, sin = jnp.cos(ang), jnp.sin(ang)
    ones = jnp.ones((seq, HEAD_DIM - ROPE_DIMS), F32)
    zeros_half = jnp.zeros((seq, half), F32)
    zeros_rest = jnp.zeros((seq, HEAD_DIM - ROPE_DIMS), F32)
    cos_t = jnp.concatenate([cos, cos, ones], axis=1)
    sin_lo_t = jnp.concatenate([zeros_half, sin, zeros_rest], axis=1)
    sin_hi_t = jnp.concatenate([-sin, zeros_half, zeros_rest], axis=1)
    return cos_t, sin_lo_t, sin_hi_t


def kernel(x, mix_norm_g, w_in, moba_out_g, sb_out_g, w_out, mlp_norm_g, w_up, w_down, final_norm_g):
    batch, seq, d_model = x.shape
    depth = w_in.shape[0]
    cos_t, sin_lo_t, sin_hi_t = _rope_tables(seq)
    x2 = x.reshape(batch * seq, d_model)
    for l in range(depth):
        qkv = _qkv_proj(x2, mix_norm_g[l][None, :], w_in[l].astype(BF16), cos_t, sin_lo_t, sin_hi_t,
                        seq=seq, tm=1024, tn=1024)
        o_a = _moba_attn(qkv, batch=batch, seq=seq)
        o_b = _sb_attn(qkv, batch=batch, seq=seq, tile=256)
        x1 = _out_proj(o_a, o_b, moba_out_g[l][None, :], sb_out_g[l][None, :],
                       w_out[l].astype(BF16), x2, tm=512)
        last = l == depth - 1
        assert last, "kernel fuses the final RMSNorm into the last layer's MLP; DEPTH must be 1"
        x2 = _mlp(x1, mlp_norm_g[l][None, :], w_up[l].astype(BF16), w_down[l].astype(BF16),
                  final_norm_g[None, :], tm=1024, tf=512)
    return x2.reshape(batch, seq, d_model)
```

```python
import functools
import math

import jax
import jax.numpy as jnp
from jax import lax
from jax.experimental import pallas as pl
from jax.experimental.pallas import tpu as pltpu

HEAD_DIM = 128
N_HEADS_MOBA = 8
N_HEADS_SB = 8
MOBA_BLOCK = 256
MOBA_TOPK = 3
ROPE_THETA = 500000.0
ROPE_DIMS = HEAD_DIM // 4
EPS = 1e-6
NEG = -1e30

F32 = jnp.float32
BF16 = jnp.bfloat16

_NT = (((1,), (1,)), ((), ()))

VMEM_LIMIT_BYTES = 56 * 1024 * 1024

Q_SCALE = HEAD_DIM ** -0.5 * math.log2(math.e)


def _rms_scale(x):
    return lax.rsqrt(jnp.mean(x * x, axis=-1, keepdims=True) + EPS)


def _emit_in_order(events):
    for _, _, thunk in sorted(events, key=lambda e: (e[0], e[1])):
        thunk()


def _qkv_kernel(x_ref, g_ref, w_ref, cos_ref, sin_lo_ref, sin_hi_ref, o_ref, h_ref):
    j = pl.program_id(1)

    @pl.when(j == 0)
    def _():
        x = x_ref[...]
        h_ref[...] = ((x * _rms_scale(x)) * g_ref[...]).astype(BF16)

    def project():
        return jnp.dot(h_ref[...], w_ref[...], preferred_element_type=F32)

    def rope(y, post_scale):
        n_heads = y.shape[1] // HEAD_DIM
        half = ROPE_DIMS // 2
        for hd in range(n_heads):
            t = y[:, hd * HEAD_DIM:(hd + 1) * HEAD_DIM]
            from_lo = pltpu.roll(t, half, 1)
            from_hi = pltpu.roll(t, HEAD_DIM - half, 1)
            r = t * cos_ref[...] + from_lo * sin_lo_ref[...] + from_hi * sin_hi_ref[...]
            if post_scale is not None:
                r = r * post_scale
            o_ref[:, hd * HEAD_DIM:(hd + 1) * HEAD_DIM] = r.astype(o_ref.dtype)

    @pl.when(j == 0)
    def _():
        rope(project(), Q_SCALE)

    @pl.when(j == 1)
    def _():
        rope(project(), None)

    @pl.when(j == 3)
    def _():
        o_ref[...] = (project() * Q_SCALE).astype(o_ref.dtype)

    @pl.when((j == 2) | (j >= 4))
    def _():
        o_ref[...] = project().astype(o_ref.dtype)


def _qkv_proj(x2, g, w_bf16, cos_t, sin_lo_t, sin_hi_t, *, seq, tm, tn):
    m, d = x2.shape
    n = w_bf16.shape[1]
    pos_blocks = seq // tm
    tab_spec = pl.BlockSpec((tm, HEAD_DIM), lambda i, j: (i % pos_blocks, 0))
    return pl.pallas_call(
        _qkv_kernel,
        out_shape=jax.ShapeDtypeStruct((m, n), BF16),
        grid=(m // tm, n // tn),
        in_specs=[
            pl.BlockSpec((tm, d), lambda i, j: (i, 0)),
            pl.BlockSpec((1, d), lambda i, j: (0, 0)),
            pl.BlockSpec((d, tn), lambda i, j: (0, j)),
            tab_spec, tab_spec, tab_spec,
        ],
        out_specs=pl.BlockSpec((tm, tn), lambda i, j: (i, j)),
        scratch_shapes=[pltpu.VMEM((tm, d), BF16)],
        compiler_params=pltpu.CompilerParams(
            dimension_semantics=("arbitrary", "arbitrary"),
            vmem_limit_bytes=VMEM_LIMIT_BYTES),
        name="qkv_proj",
    )(x2, g, w_bf16, cos_t, sin_lo_t, sin_hi_t)


_MOBA_LAG_MASK, _MOBA_LAG_EXP, _MOBA_LAG_PV = 3, 2, 2


def _moba_kernel(q_ref, k_ref, v_ref, o_ref, vt_ref, *, seq):
    blk = MOBA_BLOCK
    n_blk = seq // blk

    key_i = lax.broadcasted_iota(jnp.int32, (blk, blk), 0)
    qry_i = lax.broadcasted_iota(jnp.int32, (blk, blk), 1)
    causal = key_i <= qry_i
    blk_id = lax.broadcasted_iota(jnp.int32, (n_blk, blk), 0)

    kf = k_ref[...].astype(F32)
    k_mean = jnp.concatenate(
        [jnp.mean(kf[n * blk:(n + 1) * blk, :], axis=0, keepdims=True) for n in range(n_blk)], axis=0)
    km_hi = k_mean.astype(BF16)
    km_lo = (k_mean - km_hi.astype(F32)).astype(BF16)
    km_rows = jnp.concatenate([km_hi, km_lo], axis=0)

    blocks = [(i, j) for i in range(n_blk - 1, -1, -1) for j in ([i] + list(range(i)))]
    last_of_tile = {i: (i - 1 if i else 0) for i in range(n_blk)}
    raw, gate_raw, bias, scores, col_max, p_bf, acc, row_sum = {}, {}, {}, {}, {}, {}, {}, {}

    def score_matmul(nb):
        i, j = blocks[nb]
        q_i = q_ref[i * blk:(i + 1) * blk, :]
        k_j = k_ref[j * blk:(j + 1) * blk, :]
        if j == i:
            r = lax.dot_general(jnp.concatenate([k_j, km_rows], axis=0), q_i, _NT,
                                preferred_element_type=F32)
            raw[nb] = r[0:blk, :]
            gate_raw[i] = r[blk:blk + n_blk, :] + r[blk + n_blk:blk + 2 * n_blk, :]
        else:
            raw[nb] = lax.dot_general(k_j, q_i, _NT, preferred_element_type=F32)

    def select_blocks(i):
        past = blk_id < i
        g = jnp.where(past, gate_raw.pop(i), NEG)
        rank = jnp.zeros((n_blk, blk), jnp.int32)
        for other in range(n_blk):
            g_o = g[other:other + 1, :]
            beats = (g_o > g) | ((g_o == g) & (other < blk_id))
            rank = rank + beats.astype(jnp.int32)
        bias[i] = jnp.where(past & (rank < MOBA_TOPK), 0.0, NEG).astype(F32)

    def mask_and_max(nb):
        i, j = blocks[nb]
        s = raw.pop(nb)
        if j == i:
            s = jnp.where(causal, s, NEG)
            select_blocks(i)
        else:
            s = s + bias[i][j:j + 1, :]
        scores[nb] = s
        cm = jnp.max(s, axis=0, keepdims=True)
        col_max[i] = cm if j == i else jnp.maximum(col_max[i], cm)

    def exponentiate(nb):
        i, j = blocks[nb]
        p = jnp.exp2(scores.pop(nb) - col_max[i])
        ps = jnp.sum(p, axis=0, keepdims=True)
        row_sum[i] = ps if j == i else row_sum[i] + ps
        p_bf[nb] = p.astype(BF16)

    def value_matmul(nb):
        i, j = blocks[nb]
        d = jnp.dot(vt_ref[:, j * blk:(j + 1) * blk], p_bf.pop(nb), preferred_element_type=F32)
        acc[i] = d if j == i else acc[i] + d
        if j == last_of_tile[i]:
            o_ref[i * blk:(i + 1) * blk, :] = (acc.pop(i) / row_sum.pop(i)).T.astype(o_ref.dtype)

    def transpose_values():
        vt_ref[...] = v_ref[...].astype(F32).T.astype(BF16)

    mask_step = [nb + _MOBA_LAG_MASK for nb in range(len(blocks))]
    tile_done = {}
    for nb, (i, _) in enumerate(blocks):
        tile_done[i] = max(tile_done.get(i, 0), mask_step[nb])
    events = [(1, 1, transpose_values)]
    exp_step = -1
    for nb, (i, _) in enumerate(blocks):
        exp_step = max(exp_step + 1, tile_done[i] + _MOBA_LAG_EXP)
        events += [
            (nb, 0, functools.partial(score_matmul, nb)),
            (exp_step + _MOBA_LAG_PV, 2, functools.partial(value_matmul, nb)),
            (exp_step, 3, functools.partial(exponentiate, nb)),
            (mask_step[nb], 4, functools.partial(mask_and_max, nb)),
        ]
    _emit_in_order(events)


def _moba_attn(qkv, *, batch, seq):
    kern = functools.partial(_moba_kernel, seq=seq)
    hq, hk, hv = 0, N_HEADS_MOBA, 2 * N_HEADS_MOBA
    blk = (seq, HEAD_DIM)
    return pl.pallas_call(
        kern,
        out_shape=jax.ShapeDtypeStruct((batch * seq, N_HEADS_MOBA * HEAD_DIM), F32),
        grid=(batch, N_HEADS_MOBA),
        in_specs=[
            pl.BlockSpec(blk, lambda b, h: (b, hq + h)),
            pl.BlockSpec(blk, lambda b, h: (b, hk + h)),
            pl.BlockSpec(blk, lambda b, h: (b, hv + h)),
        ],
        out_specs=pl.BlockSpec(blk, lambda b, h: (b, h)),
        scratch_shapes=[pltpu.VMEM((HEAD_DIM, seq), BF16)],
        compiler_params=pltpu.CompilerParams(
            dimension_semantics=("arbitrary", "arbitrary"),
            vmem_limit_bytes=VMEM_LIMIT_BYTES),
        name="moba_attn",
    )(qkv, qkv, qkv)


_SB_LAGS = (1, 1, 2, 1)


def _sb_kernel(q_ref, k_ref, v_ref, o_ref, vt_ref, *, seq, tile):
    n_tiles = seq // tile
    key_i = lax.broadcasted_iota(jnp.int32, (tile, tile), 0)
    qry_i = lax.broadcasted_iota(jnp.int32, (tile, tile), 1)
    causal = key_i < qry_i
    this_or_later = (qry_i >= key_i).astype(BF16)

    blocks = [(i, j) for i in range(n_tiles - 1, -1, -1) for j in range(i, -1, -1)]
    raw, logit, soft_bf, later_sum, a_bf, acc, carry = {}, {}, {}, {}, {}, {}, {}

    def logit_matmul(nb):
        i, j = blocks[nb]
        raw[nb] = lax.dot_general(k_ref[j * tile:(j + 1) * tile, :], q_ref[i * tile:(i + 1) * tile, :],
                                  _NT, preferred_element_type=F32)

    def softplus(nb):
        i, j = blocks[nb]
        z = raw.pop(nb)
        t = jnp.maximum(z, 0.0) + jnp.log2(1.0 + jnp.exp2(-jnp.abs(z)))
        if j == i:
            t = jnp.where(causal, t, 0.0)
        logit[nb] = z
        soft_bf[nb] = t.astype(BF16)

    def cumsum_matmul(nb):
        later_sum[nb] = jnp.dot(this_or_later, soft_bf.pop(nb), preferred_element_type=F32)

    def weights(nb):
        i, j = blocks[nb]
        inc = later_sum.pop(nb)
        x = logit.pop(nb) - inc
        if j != i:
            x = x - carry[i]
        a = jnp.exp2(x)
        if j == i:
            a = jnp.where(causal, a, 0.0)
        total = inc[0:1, :]
        carry[i] = total if j == i else carry[i] + total
        a_bf[nb] = a.astype(BF16)

    def value_matmul(nb):
        i, j = blocks[nb]
        d = jnp.dot(vt_ref[:, j * tile:(j + 1) * tile], a_bf.pop(nb), preferred_element_type=F32)
        acc[i] = d if j == i else acc[i] + d
        if j == 0:
            o_ref[i * tile:(i + 1) * tile, :] = acc.pop(i).T.astype(o_ref.dtype)

    def transpose_values():
        vt_ref[...] = v_ref[...].astype(F32).T.astype(BF16)

    l_soft, l_cum, l_w, l_pv = _SB_LAGS
    events = [(1, 1, transpose_values)]
    for nb in range(len(blocks)):
        events += [
            (nb, 0, functools.partial(logit_matmul, nb)),
            (nb + l_soft + l_cum + l_w + l_pv, 2, functools.partial(value_matmul, nb)),
            (nb + l_soft + l_cum, 3, functools.partial(cumsum_matmul, nb)),
            (nb + l_soft + l_cum + l_w, 4, functools.partial(weights, nb)),
            (nb + l_soft, 5, functools.partial(softplus, nb)),
        ]
    _emit_in_order(events)


def _sb_attn(qkv, *, batch, seq, tile):
    kern = functools.partial(_sb_kernel, seq=seq, tile=tile)
    base = 3 * N_HEADS_MOBA
    hq, hk, hv = base, base + N_HEADS_SB, base + 2 * N_HEADS_SB
    blk = (seq, HEAD_DIM)
    return pl.pallas_call(
        kern,
        out_shape=jax.ShapeDtypeStruct((batch * seq, N_HEADS_SB * HEAD_DIM), F32),
        grid=(batch, N_HEADS_SB),
        in_specs=[
            pl.BlockSpec(blk, lambda b, h: (b, hq + h)),
            pl.BlockSpec(blk, lambda b, h: (b, hk + h)),
            pl.BlockSpec(blk, lambda b, h: (b, hv + h)),
        ],
        out_specs=pl.BlockSpec(blk, lambda b, h: (b, h)),
        scratch_shapes=[pltpu.VMEM((HEAD_DIM, seq), BF16)],
        compiler_params=pltpu.CompilerParams(
            dimension_semantics=("arbitrary", "arbitrary"),
            vmem_limit_bytes=VMEM_LIMIT_BYTES),
        name="sb_attn",
    )(qkv, qkv, qkv)


def _out_proj_kernel(oa_ref, ob_ref, ga_ref, gb_ref, w_ref, x_ref, y_ref):
    d_a = oa_ref.shape[1]
    oa = oa_ref[...]
    ob = ob_ref[...]
    na = ((oa * _rms_scale(oa)) * ga_ref[...]).astype(BF16)
    nb = ((ob * _rms_scale(ob)) * gb_ref[...]).astype(BF16)
    y = (jnp.dot(na, w_ref[0:d_a, :], preferred_element_type=F32)
         + jnp.dot(nb, w_ref[d_a:, :], preferred_element_type=F32))
    y_ref[...] = x_ref[...] + y


def _out_proj(o_a, o_b, g_a, g_b, w_bf16, x2, *, tm):
    m, d_a = o_a.shape
    d_b = o_b.shape[1]
    d = w_bf16.shape[1]
    return pl.pallas_call(
        _out_proj_kernel,
        out_shape=jax.ShapeDtypeStruct((m, d), F32),
        grid=(m // tm,),
        in_specs=[
            pl.BlockSpec((tm, d_a), lambda i: (i, 0)),
            pl.BlockSpec((tm, d_b), lambda i: (i, 0)),
            pl.BlockSpec((1, d_a), lambda i: (0, 0)),
            pl.BlockSpec((1, d_b), lambda i: (0, 0)),
            pl.BlockSpec((d_a + d_b, d), lambda i: (0, 0)),
            pl.BlockSpec((tm, d), lambda i: (i, 0)),
        ],
        out_specs=pl.BlockSpec((tm, d), lambda i: (i, 0)),
        compiler_params=pltpu.CompilerParams(
            dimension_semantics=("arbitrary",),
            vmem_limit_bytes=VMEM_LIMIT_BYTES),
        name="out_proj",
    )(o_a, o_b, g_a, g_b, w_bf16, x2)


def _mlp_kernel(x_ref, g_ref, wu_ref, wd_ref, gf_ref, o_ref, h_ref, *, n_chunk):
    f = pl.program_id(1)
    n_f = pl.num_programs(1)

    @pl.when(f == 0)
    def _():
        x = x_ref[...]
        h_ref[...] = ((x * _rms_scale(x)) * g_ref[...]).astype(BF16)

    u = jnp.dot(h_ref[...], wu_ref[...], preferred_element_type=F32)
    r = jnp.maximum(u, 0.0)
    act = (r * r).astype(BF16)
    d = o_ref.shape[1]
    cw = d // n_chunk

    @pl.when(f == 0)
    def _():
        for c in range(n_chunk):
            o_ref[:, c * cw:(c + 1) * cw] = jnp.dot(
                act, wd_ref[:, c * cw:(c + 1) * cw], preferred_element_type=F32)

    @pl.when(f > 0)
    def _():
        for c in range(n_chunk):
            o_ref[:, c * cw:(c + 1) * cw] += jnp.dot(
                act, wd_ref[:, c * cw:(c + 1) * cw], preferred_element_type=F32)

    @pl.when(f == n_f - 1)
    def _():
        y = x_ref[...] + o_ref[...]
        o_ref[...] = (y * _rms_scale(y)) * gf_ref[...]


def _mlp(x1, g, wu_bf16, wd_bf16, g_final, *, tm, tf):
    m, d = x1.shape
    d_ff = wu_bf16.shape[1]
    kern = functools.partial(_mlp_kernel, n_chunk=4)
    return pl.pallas_call(
        kern,
        out_shape=jax.ShapeDtypeStruct((m, d), F32),
        grid=(m // tm, d_ff // tf),
        in_specs=[
            pl.BlockSpec((tm, d), lambda i, f: (i, 0)),
            pl.BlockSpec((1, d), lambda i, f: (0, 0)),
            pl.BlockSpec((d, tf), lambda i, f: (0, f)),
            pl.BlockSpec((tf, d), lambda i, f: (f, 0)),
            pl.BlockSpec((1, d), lambda i, f: (0, 0)),
        ],
        out_specs=pl.BlockSpec((tm, d), lambda i, f: (i, 0)),
        scratch_shapes=[pltpu.VMEM((tm, d), BF16)],
        compiler_params=pltpu.CompilerParams(
            dimension_semantics=("arbitrary", "arbitrary"),
            vmem_limit_bytes=VMEM_LIMIT_BYTES),
        name="mlp",
    )(x1, g, wu_bf16, wd_bf16, g_final)


def _rope_tables(seq):
    half = ROPE_DIMS // 2
    inv_freq = ROPE_THETA ** (-jnp.arange(half, dtype=F32) / half)
    ang = jnp.arange(seq, dtype=F32)[:, None] * inv_freq[None, :]
    cos, sin = jnp.cos(ang), jnp.sin(ang)
    ones = jnp.ones((seq, HEAD_DIM - ROPE_DIMS), F32)
    zeros_half = jnp.zeros((seq, half), F32)
    zeros_rest = jnp.zeros((seq, HEAD_DIM - ROPE_DIMS), F32)
    cos_t = jnp.concatenate([cos, cos, ones], axis=1)
    sin_lo_t = jnp.concatenate([zeros_half, sin, zeros_rest], axis=1)
    sin_hi_t = jnp.concatenate([-sin, zeros_half, zeros_rest], axis=1)
    return cos_t, sin_lo_t, sin_hi_t


def kernel(x, mix_norm_g, w_in, moba_out_g, sb_out_g, w_out, mlp_norm_g, w_up, w_down, final_norm_g):
    batch, seq, d_model = x.shape
    depth = w_in.shape[0]
    cos_t, sin_lo_t, sin_hi_t = _rope_tables(seq)
    x2 = x.reshape(batch * seq, d_model)
    for l in range(depth):
        qkv = _qkv_proj(x2, mix_norm_g[l][None, :], w_in[l].astype(BF16), cos_t, sin_lo_t, sin_hi_t,
                        seq=seq, tm=1024, tn=1024)
        o_a = _moba_attn(qkv, batch=batch, seq=seq)
        o_b = _sb_attn(qkv, batch=batch, seq=seq, tile=256)
        x1 = _out_proj(o_a, o_b, moba_out_g[l][None, :], sb_out_g[l][None, :],
                       w_out[l].astype(BF16), x2, tm=512)
        last = l == depth - 1
        assert last, "kernel fuses the final RMSNorm into the last layer's MLP; DEPTH must be 1"
        x2 = _mlp(x1, mlp_norm_g[l][None, :], w_up[l].astype(BF16), w_down[l].astype(BF16),
                  final_norm_g[None, :], tm=1024, tf=512)
    return x2.reshape(batch, seq, d_model)
```

```python
import functools
import math

import jax
import jax.numpy as jnp
from jax import lax
from jax.experimental import pallas as pl
from jax.experimental.pallas import tpu as pltpu

HEAD_DIM = 128
N_HEADS_MOBA = 8
N_HEADS_SB = 8
MOBA_BLOCK = 256
MOBA_TOPK = 3
ROPE_THETA = 500000.0
ROPE_DIMS = HEAD_DIM // 4
EPS = 1e-6
NEG = -1e30

F32 = jnp.float32
BF16 = jnp.bfloat16

_NT = (((1,), (1,)), ((), ()))

VMEM_LIMIT_BYTES = 56 * 1024 * 1024

Q_SCALE = HEAD_DIM ** -0.5 * math.log2(math.e)


def _rms_scale(x):
    return lax.rsqrt(jnp.mean(x * x, axis=-1, keepdims=True) + EPS)


def _emit_in_order(events):
    for _, _, thunk in sorted(events, key=lambda e: (e[0], e[1])):
        thunk()


def _qkv_kernel(x_ref, g_ref, w_ref, cos_ref, sin_lo_ref, sin_hi_ref, o_ref, h_ref):
    j = pl.program_id(1)

    @pl.when(j == 0)
    def _():
        x = x_ref[...]
        h_ref[...] = ((x * _rms_scale(x)) * g_ref[...]).astype(BF16)

    def project():
        return jnp.dot(h_ref[...], w_ref[...], preferred_element_type=F32)

    def rope(y, post_scale):
        n_heads = y.shape[1] // HEAD_DIM
        half = ROPE_DIMS // 2
        lane = lax.broadcasted_iota(jnp.int32, (y.shape[0], HEAD_DIM), 1)
        partner = jnp.where(lane < ROPE_DIMS, lane ^ half, lane)
        for hd in range(n_heads):
            t = y[:, hd * HEAD_DIM:(hd + 1) * HEAD_DIM]
            swapped = jnp.take_along_axis(t, partner, axis=1)
            r = t * cos_ref[...] + swapped * (sin_lo_ref[...] + sin_hi_ref[...])
            if post_scale is not None:
                r = r * post_scale
            o_ref[:, hd * HEAD_DIM:(hd + 1) * HEAD_DIM] = r.astype(o_ref.dtype)

    @pl.when(j == 0)
    def _():
        rope(project(), Q_SCALE)

    @pl.when(j == 1)
    def _():
        rope(project(), None)

    @pl.when(j == 3)
    def _():
        o_ref[...] = (project() * Q_SCALE).astype(o_ref.dtype)

    @pl.when((j == 2) | (j >= 4))
    def _():
        o_ref[...] = project().astype(o_ref.dtype)


def _qkv_proj(x2, g, w_bf16, cos_t, sin_lo_t, sin_hi_t, *, seq, tm, tn):
    m, d = x2.shape
    n = w_bf16.shape[1]
    pos_blocks = seq // tm
    tab_spec = pl.BlockSpec((tm, HEAD_DIM), lambda i, j: (i % pos_blocks, 0))
    return pl.pallas_call(
        _qkv_kernel,
        out_shape=jax.ShapeDtypeStruct((m, n), BF16),
        grid=(m // tm, n // tn),
        in_specs=[
            pl.BlockSpec((tm, d), lambda i, j: (i, 0)),
            pl.BlockSpec((1, d), lambda i, j: (0, 0)),
            pl.BlockSpec((d, tn), lambda i, j: (0, j)),
            tab_spec, tab_spec, tab_spec,
        ],
        out_specs=pl.BlockSpec((tm, tn), lambda i, j: (i, j)),
        scratch_shapes=[pltpu.VMEM((tm, d), BF16)],
        compiler_params=pltpu.CompilerParams(
            dimension_semantics=("arbitrary", "arbitrary"),
            vmem_limit_bytes=VMEM_LIMIT_BYTES),
        name="qkv_proj",
    )(x2, g, w_bf16, cos_t, sin_lo_t, sin_hi_t)


_MOBA_LAG_MASK, _MOBA_LAG_EXP, _MOBA_LAG_PV = 3, 2, 2


def _moba_kernel(q_ref, k_ref, v_ref, o_ref, vt_ref, *, seq):
    blk = MOBA_BLOCK
    n_blk = seq // blk

    key_i = lax.broadcasted_iota(jnp.int32, (blk, blk), 0)
    qry_i = lax.broadcasted_iota(jnp.int32, (blk, blk), 1)
    causal = key_i <= qry_i
    blk_id = lax.broadcasted_iota(jnp.int32, (n_blk, blk), 0)

    kf = k_ref[...].astype(F32)
    k_mean = jnp.concatenate(
        [jnp.mean(kf[n * blk:(n + 1) * blk, :], axis=0, keepdims=True) for n in range(n_blk)], axis=0)
    km_hi = k_mean.astype(BF16)
    km_lo = (k_mean - km_hi.astype(F32)).astype(BF16)
    km_rows = jnp.concatenate([km_hi, km_lo], axis=0)

    blocks = [(i, j) for i in range(n_blk - 1, -1, -1) for j in ([i] + list(range(i)))]
    last_of_tile = {i: (i - 1 if i else 0) for i in range(n_blk)}
    raw, gate_raw, bias, scores, col_max, p_bf, acc, row_sum = {}, {}, {}, {}, {}, {}, {}, {}

    def score_matmul(nb):
        i, j = blocks[nb]
        q_i = q_ref[i * blk:(i + 1) * blk, :]
        k_j = k_ref[j * blk:(j + 1) * blk, :]
        if j == i:
            r = lax.dot_general(jnp.concatenate([k_j, km_rows], axis=0), q_i, _NT,
                                preferred_element_type=F32)
            raw[nb] = r[0:blk, :]
            gate_raw[i] = r[blk:blk + n_blk, :] + r[blk + n_blk:blk + 2 * n_blk, :]
        else:
            raw[nb] = lax.dot_general(k_j, q_i, _NT, preferred_element_type=F32)

    def select_blocks(i):
        past = blk_id < i
        g = jnp.where(past, gate_raw.pop(i), NEG)
        rank = jnp.zeros((n_blk, blk), jnp.int32)
        for other in range(n_blk):
            g_o = g[other:other + 1, :]
            beats = (g_o > g) | ((g_o == g) & (other < blk_id))
            rank = rank + beats.astype(jnp.int32)
        bias[i] = jnp.where(past & (rank < MOBA_TOPK), 0.0, NEG).astype(F32)

    def mask_and_max(nb):
        i, j = blocks[nb]
        s = raw.pop(nb)
        if j == i:
            s = jnp.where(causal, s, NEG)
            select_blocks(i)
        else:
            s = s + bias[i][j:j + 1, :]
        scores[nb] = s
        cm = jnp.max(s, axis=0, keepdims=True)
        col_max[i] = cm if j == i else jnp.maximum(col_max[i], cm)

    def exponentiate(nb):
        i, j = blocks[nb]
        p = jnp.exp2(scores.pop(nb) - col_max[i])
        ps = jnp.sum(p, axis=0, keepdims=True)
        row_sum[i] = ps if j == i else row_sum[i] + ps
        p_bf[nb] = p.astype(BF16)

    def value_matmul(nb):
        i, j = blocks[nb]
        d = jnp.dot(vt_ref[:, j * blk:(j + 1) * blk], p_bf.pop(nb), preferred_element_type=F32)
        acc[i] = d if j == i else acc[i] + d
        if j == last_of_tile[i]:
            o_ref[i * blk:(i + 1) * blk, :] = (acc.pop(i) / row_sum.pop(i)).T.astype(o_ref.dtype)

    def transpose_values():
        vt_ref[...] = v_ref[...].astype(F32).T.astype(BF16)

    mask_step = [nb + _MOBA_LAG_MASK for nb in range(len(blocks))]
    tile_done = {}
    for nb, (i, _) in enumerate(blocks):
        tile_done[i] = max(tile_done.get(i, 0), mask_step[nb])
    events = [(1, 1, transpose_values)]
    exp_step = -1
    for nb, (i, _) in enumerate(blocks):
        exp_step = max(exp_step + 1, tile_done[i] + _MOBA_LAG_EXP)
        events += [
            (nb, 0, functools.partial(score_matmul, nb)),
            (exp_step + _MOBA_LAG_PV, 2, functools.partial(value_matmul, nb)),
            (exp_step, 3, functools.partial(exponentiate, nb)),
            (mask_step[nb], 4, functools.partial(mask_and_max, nb)),
        ]
    _emit_in_order(events)


def _moba_attn(qkv, *, batch, seq):
    kern = functools.partial(_moba_kernel, seq=seq)
    hq, hk, hv = 0, N_HEADS_MOBA, 2 * N_HEADS_MOBA
    blk = (seq, HEAD_DIM)
    return pl.pallas_call(
        kern,
        out_shape=jax.ShapeDtypeStruct((batch * seq, N_HEADS_MOBA * HEAD_DIM), F32),
        grid=(batch, N_HEADS_MOBA),
        in_specs=[
            pl.BlockSpec(blk, lambda b, h: (b, hq + h)),
            pl.BlockSpec(blk, lambda b, h: (b, hk + h)),
            pl.BlockSpec(blk, lambda b, h: (b, hv + h)),
        ],
        out_specs=pl.BlockSpec(blk, lambda b, h: (b, h)),
        scratch_shapes=[pltpu.VMEM((HEAD_DIM, seq), BF16)],
        compiler_params=pltpu.CompilerParams(
            dimension_semantics=("arbitrary", "arbitrary"),
            vmem_limit_bytes=VMEM_LIMIT_BYTES),
        name="moba_attn",
    )(qkv, qkv, qkv)


_SB_LAGS = (1, 1, 2, 1)


def _sb_kernel(q_ref, k_ref, v_ref, o_ref, vt_ref, *, seq, tile):
    n_tiles = seq // tile
    key_i = lax.broadcasted_iota(jnp.int32, (tile, tile), 0)
    qry_i = lax.broadcasted_iota(jnp.int32, (tile, tile), 1)
    causal = key_i < qry_i
    this_or_later = (qry_i >= key_i).astype(BF16)

    blocks = [(i, j) for i in range(n_tiles - 1, -1, -1) for j in range(i, -1, -1)]
    raw, logit, soft_bf, later_sum, a_bf, acc, carry = {}, {}, {}, {}, {}, {}, {}

    def logit_matmul(nb):
        i, j = blocks[nb]
        raw[nb] = lax.dot_general(k_ref[j * tile:(j + 1) * tile, :], q_ref[i * tile:(i + 1) * tile, :],
                                  _NT, preferred_element_type=F32)

    def softplus(nb):
        i, j = blocks[nb]
        z = raw.pop(nb)
        t = jnp.maximum(z, 0.0) + jnp.log2(1.0 + jnp.exp2(-jnp.abs(z)))
        if j == i:
            t = jnp.where(causal, t, 0.0)
        logit[nb] = z
        soft_bf[nb] = t.astype(BF16)

    def cumsum_matmul(nb):
        later_sum[nb] = jnp.dot(this_or_later, soft_bf.pop(nb), preferred_element_type=F32)

    def weights(nb):
        i, j = blocks[nb]
        inc = later_sum.pop(nb)
        x = logit.pop(nb) - inc
        if j != i:
            x = x - carry[i]
        a = jnp.exp2(x)
        if j == i:
            a = jnp.where(causal, a, 0.0)
        total = inc[0:1, :]
        carry[i] = total if j == i else carry[i] + total
        a_bf[nb] = a.astype(BF16)

    def value_matmul(nb):
        i, j = blocks[nb]
        d = jnp.dot(vt_ref[:, j * tile:(j + 1) * tile], a_bf.pop(nb), preferred_element_type=F32)
        acc[i] = d if j == i else acc[i] + d
        if j == 0:
            o_ref[i * tile:(i + 1) * tile, :] = acc.pop(i).T.astype(o_ref.dtype)

    def transpose_values():
        vt_ref[...] = v_ref[...].astype(F32).T.astype(BF16)

    l_soft, l_cum, l_w, l_pv = _SB_LAGS
    events = [(1, 1, transpose_values)]
    for nb in range(len(blocks)):
        events += [
            (nb, 0, functools.partial(logit_matmul, nb)),
            (nb + l_soft + l_cum + l_w + l_pv, 2, functools.partial(value_matmul, nb)),
            (nb + l_soft + l_cum, 3, functools.partial(cumsum_matmul, nb)),
            (nb + l_soft + l_cum + l_w, 4, functools.partial(weights, nb)),
            (nb + l_soft, 5, functools.partial(softplus, nb)),
        ]
    _emit_in_order(events)


def _sb_attn(qkv, *, batch, seq, tile):
    kern = functools.partial(_sb_kernel, seq=seq, tile=tile)
    base = 3 * N_HEADS_MOBA
    hq, hk, hv = base, base + N_HEADS_SB, base + 2 * N_HEADS_SB
    blk = (seq, HEAD_DIM)
    return pl.pallas_call(
        kern,
        out_shape=jax.ShapeDtypeStruct((batch * seq, N_HEADS_SB * HEAD_DIM), F32),
        grid=(batch, N_HEADS_SB),
        in_specs=[
            pl.BlockSpec(blk, lambda b, h: (b, hq + h)),
            pl.BlockSpec(blk, lambda b, h: (b, hk + h)),
            pl.BlockSpec(blk, lambda b, h: (b, hv + h)),
        ],
        out_specs=pl.BlockSpec(blk, lambda b, h: (b, h)),
        scratch_shapes=[pltpu.VMEM((HEAD_DIM, seq), BF16)],
        compiler_params=pltpu.CompilerParams(
            dimension_semantics=("arbitrary", "arbitrary"),
            vmem_limit_bytes=VMEM_LIMIT_BYTES),
        name="sb_attn",
    )(qkv, qkv, qkv)


def _out_proj_kernel(oa_ref, ob_ref, ga_ref, gb_ref, w_ref, x_ref, y_ref):
    d_a = oa_ref.shape[1]
    oa = oa_ref[...]
    ob = ob_ref[...]
    na = ((oa * _rms_scale(oa)) * ga_ref[...]).astype(BF16)
    nb = ((ob * _rms_scale(ob)) * gb_ref[...]).astype(BF16)
    y = (jnp.dot(na, w_ref[0:d_a, :], preferred_element_type=F32)
         + jnp.dot(nb, w_ref[d_a:, :], preferred_element_type=F32))
    y_ref[...] = x_ref[...] + y


def _out_proj(o_a, o_b, g_a, g_b, w_bf16, x2, *, tm):
    m, d_a = o_a.shape
    d_b = o_b.shape[1]
    d = w_bf16.shape[1]
    return pl.pallas_call(
        _out_proj_kernel,
        out_shape=jax.ShapeDtypeStruct((m, d), F32),
        grid=(m // tm,),
        in_specs=[
            pl.BlockSpec((tm, d_a), lambda i: (i, 0)),
            pl.BlockSpec((tm, d_b), lambda i: (i, 0)),
            pl.BlockSpec((1, d_a), lambda i: (0, 0)),
            pl.BlockSpec((1, d_b), lambda i: (0, 0)),
            pl.BlockSpec((d_a + d_b, d), lambda i: (0, 0)),
            pl.BlockSpec((tm, d), lambda i: (i, 0)),
        ],
        out_specs=pl.BlockSpec((tm, d), lambda i: (i, 0)),
        compiler_params=pltpu.CompilerParams(
            dimension_semantics=("arbitrary",),
            vmem_limit_bytes=VMEM_LIMIT_BYTES),
        name="out_proj",
    )(o_a, o_b, g_a, g_b, w_bf16, x2)


def _mlp_kernel(x_ref, g_ref, wu_ref, wd_ref, gf_ref, o_ref, h_ref, *, n_chunk):
    f = pl.program_id(1)
    n_f = pl.num_programs(1)

    @pl.when(f == 0)
    def _():
        x = x_ref[...]
        h_ref[...] = ((x * _rms_scale(x)) * g_ref[...]).astype(BF16)
        o_ref[...] = x

    u = jnp.dot(h_ref[...], wu_ref[...], preferred_element_type=F32)
    r = jnp.maximum(u, 0.0)
    act = (r * r).astype(BF16)
    d = o_ref.shape[1]
    cw = d // n_chunk
    for c in range(n_chunk):
        o_ref[:, c * cw:(c + 1) * cw] += jnp.dot(
            act, wd_ref[:, c * cw:(c + 1) * cw], preferred_element_type=F32)

    @pl.when(f == n_f - 1)
    def _():
        y = o_ref[...]
        o_ref[...] = (y * _rms_scale(y)) * gf_ref[...]


def _mlp(x1, g, wu_bf16, wd_bf16, g_final, *, tm, tf):
    m, d = x1.shape
    d_ff = wu_bf16.shape[1]
    kern = functools.partial(_mlp_kernel, n_chunk=4)
    return pl.pallas_call(
        kern,
        out_shape=jax.ShapeDtypeStruct((m, d), F32),
        grid=(m // tm, d_ff // tf),
        in_specs=[
            pl.BlockSpec((tm, d), lambda i, f: (i, 0)),
            pl.BlockSpec((1, d), lambda i, f: (0, 0)),
            pl.BlockSpec((d, tf), lambda i, f: (0, f)),
            pl.BlockSpec((tf, d), lambda i, f: (f, 0)),
            pl.BlockSpec((1, d), lambda i, f: (0, 0)),
        ],
        out_specs=pl.BlockSpec((tm, d), lambda i, f: (i, 0)),
        scratch_shapes=[pltpu.VMEM((tm, d), BF16)],
        compiler_params=pltpu.CompilerParams(
            dimension_semantics=("arbitrary", "arbitrary"),
            vmem_limit_bytes=VMEM_LIMIT_BYTES),
        name="mlp",
    )(x1, g, wu_bf16, wd_bf16, g_final)


def _rope_tables(seq):
    half = ROPE_DIMS // 2
    inv_freq = ROPE_THETA ** (-jnp.arange(half, dtype=F32) / half)
    ang = jnp.arange(seq, dtype=F32)[:, None] * inv_freq[None, :]
    cos, sin = jnp.cos(ang), jnp.sin(ang)
    ones = jnp.ones((seq, HEAD_DIM - ROPE_DIMS), F32)
    zeros_half = jnp.zeros((seq, half), F32)
    zeros_rest = jnp.zeros((seq, HEAD_DIM - ROPE_DIMS), F32)
    cos_t = jnp.concatenate([cos, cos, ones], axis=1)
    sin_lo_t = jnp.concatenate([zeros_half, sin, zeros_rest], axis=1)
    sin_hi_t = jnp.concatenate([-sin, zeros_half, zeros_rest], axis=1)
    return cos_t, sin_lo_t, sin_hi_t


def kernel(x, mix_norm_g, w_in, moba_out_g, sb_out_g, w_out, mlp_norm_g, w_up, w_down, final_norm_g):
    batch, seq, d_model = x.shape
    depth = w_in.shape[0]
    cos_t, sin_lo_t, sin_hi_t = _rope_tables(seq)
    x2 = x.reshape(batch * seq, d_model)
    for l in range(depth):
        qkv = _qkv_proj(x2, mix_norm_g[l][None, :], w_in[l].astype(BF16), cos_t, sin_lo_t, sin_hi_t,
                        seq=seq, tm=1024, tn=1024)
        o_a = _moba_attn(qkv, batch=batch, seq=seq)
        o_b = _sb_attn(qkv, batch=batch, seq=seq, tile=256)
        x1 = _out_proj(o_a, o_b, moba_out_g[l][None, :], sb_out_g[l][None, :],
                       w_out[l].astype(BF16), x2, tm=512)
        last = l == depth - 1
        assert last, "kernel fuses the final RMSNorm into the last layer's MLP; DEPTH must be 1"
        x2 = _mlp(x1, mlp_norm_g[l][None, :], w_up[l].astype(BF16), w_down[l].astype(BF16),
                  final_norm_g[None, :], tm=1024, tf=512)
    return x2.reshape(batch, seq, d_model)
```

```python
import functools
import math

import jax
import jax.numpy as jnp
from jax import lax
from jax.experimental import pallas as pl
from jax.experimental.pallas import tpu as pltpu

HEAD_DIM = 128
N_HEADS_MOBA = 8
N_HEADS_SB = 8
MOBA_BLOCK = 256
MOBA_TOPK = 3
ROPE_THETA = 500000.0
ROPE_DIMS = HEAD_DIM // 4
EPS = 1e-6
NEG = -1e30

F32 = jnp.float32
BF16 = jnp.bfloat16

_NT = (((1,), (1,)), ((), ()))

VMEM_LIMIT_BYTES = 56 * 1024 * 1024

Q_SCALE = HEAD_DIM ** -0.5 * math.log2(math.e)


def _rms_scale(x):
    return lax.rsqrt(jnp.mean(x * x, axis=-1, keepdims=True) + EPS)


def _emit_in_order(events):
    for _, _, thunk in sorted(events, key=lambda e: (e[0], e[1])):
        thunk()


def _qkv_kernel(x_ref, g_ref, w_ref, cos_ref, sin_lo_ref, sin_hi_ref, side_ref, o_ref, side_bf_ref, h_ref):
    j = pl.program_id(1)

    @pl.when(j == 0)
    def _():
        x = x_ref[...]
        h_ref[...] = ((x * _rms_scale(x)) * g_ref[...]).astype(BF16)
        side_bf_ref[...] = side_ref[...].astype(BF16)

    def project():
        return jnp.dot(h_ref[...], w_ref[...].astype(BF16), preferred_element_type=F32)

    def rope(y, post_scale):
        n_heads = y.shape[1] // HEAD_DIM
        half = ROPE_DIMS // 2
        lane = lax.broadcasted_iota(jnp.int32, (y.shape[0], HEAD_DIM), 1)
        partner = jnp.where(lane < ROPE_DIMS, lane ^ half, lane)
        for hd in range(n_heads):
            t = y[:, hd * HEAD_DIM:(hd + 1) * HEAD_DIM]
            swapped = jnp.take_along_axis(t, partner, axis=1)
            r = t * cos_ref[...] + swapped * (sin_lo_ref[...] + sin_hi_ref[...])
            if post_scale is not None:
                r = r * post_scale
            o_ref[:, hd * HEAD_DIM:(hd + 1) * HEAD_DIM] = r.astype(o_ref.dtype)

    @pl.when(j == 0)
    def _():
        rope(project(), Q_SCALE)

    @pl.when(j == 1)
    def _():
        rope(project(), None)

    @pl.when(j == 3)
    def _():
        o_ref[...] = (project() * Q_SCALE).astype(o_ref.dtype)

    @pl.when((j == 2) | (j >= 4))
    def _():
        o_ref[...] = project().astype(o_ref.dtype)


def _side_cast_specs(side, n_steps, index_map):
    rows, cols = side.shape
    slab = rows // n_steps
    assert slab * n_steps == rows and slab % 16 == 0, (rows, n_steps)
    spec = pl.BlockSpec((slab, cols), index_map)
    return spec, spec, jax.ShapeDtypeStruct((rows, cols), BF16)


def _qkv_proj(x2, g, w, cos_t, sin_lo_t, sin_hi_t, side, *, seq, tm, tn):
    m, d = x2.shape
    n = w.shape[1]
    pos_blocks = seq // tm
    tab_spec = pl.BlockSpec((tm, HEAD_DIM), lambda i, j: (i % pos_blocks, 0))
    side_in, side_out, side_shape = _side_cast_specs(side, m // tm, lambda i, j: (i, 0))
    return pl.pallas_call(
        _qkv_kernel,
        out_shape=(jax.ShapeDtypeStruct((m, n), BF16), side_shape),
        grid=(m // tm, n // tn),
        in_specs=[
            pl.BlockSpec((tm, d), lambda i, j: (i, 0)),
            pl.BlockSpec((1, d), lambda i, j: (0, 0)),
            pl.BlockSpec((d, tn), lambda i, j: (0, j)),
            tab_spec, tab_spec, tab_spec,
            side_in,
        ],
        out_specs=(pl.BlockSpec((tm, tn), lambda i, j: (i, j)), side_out),
        scratch_shapes=[pltpu.VMEM((tm, d), BF16)],
        compiler_params=pltpu.CompilerParams(
            dimension_semantics=("arbitrary", "arbitrary"),
            vmem_limit_bytes=VMEM_LIMIT_BYTES),
        name="qkv_proj",
    )(x2, g, w, cos_t, sin_lo_t, sin_hi_t, side)


_MOBA_LAG_MASK, _MOBA_LAG_EXP, _MOBA_LAG_PV = 3, 2, 2


def _moba_kernel(q_ref, k_ref, v_ref, side_ref, o_ref, side_bf_ref, vt_ref, *, seq):
    blk = MOBA_BLOCK
    n_blk = seq // blk

    key_i = lax.broadcasted_iota(jnp.int32, (blk, blk), 0)
    qry_i = lax.broadcasted_iota(jnp.int32, (blk, blk), 1)
    causal = key_i <= qry_i
    blk_id = lax.broadcasted_iota(jnp.int32, (n_blk, blk), 0)

    kf = k_ref[...].astype(F32)
    k_mean = jnp.concatenate(
        [jnp.mean(kf[n * blk:(n + 1) * blk, :], axis=0, keepdims=True) for n in range(n_blk)], axis=0)
    km_hi = k_mean.astype(BF16)
    km_lo = (k_mean - km_hi.astype(F32)).astype(BF16)
    km_rows = jnp.concatenate([km_hi, km_lo], axis=0)

    blocks = [(i, j) for i in range(n_blk - 1, -1, -1) for j in ([i] + list(range(i)))]
    last_of_tile = {i: (i - 1 if i else 0) for i in range(n_blk)}
    raw, gate_raw, bias, scores, col_max, p_bf, acc, row_sum = {}, {}, {}, {}, {}, {}, {}, {}

    def score_matmul(nb):
        i, j = blocks[nb]
        q_i = q_ref[i * blk:(i + 1) * blk, :]
        k_j = k_ref[j * blk:(j + 1) * blk, :]
        if j == i:
            r = lax.dot_general(jnp.concatenate([k_j, km_rows], axis=0), q_i, _NT,
                                preferred_element_type=F32)
            raw[nb] = r[0:blk, :]
            gate_raw[i] = r[blk:blk + n_blk, :] + r[blk + n_blk:blk + 2 * n_blk, :]
        else:
            raw[nb] = lax.dot_general(k_j, q_i, _NT, preferred_element_type=F32)

    def select_blocks(i):
        past = blk_id < i
        g = jnp.where(past, gate_raw.pop(i), NEG)
        rank = jnp.zeros((n_blk, blk), jnp.int32)
        for other in range(n_blk):
            g_o = g[other:other + 1, :]
            beats = (g_o > g) | ((g_o == g) & (other < blk_id))
            rank = rank + beats.astype(jnp.int32)
        bias[i] = jnp.where(past & (rank < MOBA_TOPK), 0.0, NEG).astype(F32)

    def mask_and_max(nb):
        i, j = blocks[nb]
        s = raw.pop(nb)
        if j == i:
            s = jnp.where(causal, s, NEG)
            select_blocks(i)
        else:
            s = s + bias[i][j:j + 1, :]
        scores[nb] = s
        cm = jnp.max(s, axis=0, keepdims=True)
        col_max[i] = cm if j == i else jnp.maximum(col_max[i], cm)

    def exponentiate(nb):
        i, j = blocks[nb]
        p = jnp.exp2(scores.pop(nb) - col_max[i])
        ps = jnp.sum(p, axis=0, keepdims=True)
        row_sum[i] = ps if j == i else row_sum[i] + ps
        p_bf[nb] = p.astype(BF16)

    def value_matmul(nb):
        i, j = blocks[nb]
        d = jnp.dot(vt_ref[:, j * blk:(j + 1) * blk], p_bf.pop(nb), preferred_element_type=F32)
        acc[i] = d if j == i else acc[i] + d
        if j == last_of_tile[i]:
            o_ref[i * blk:(i + 1) * blk, :] = (acc.pop(i) / row_sum.pop(i)).T.astype(o_ref.dtype)

    def transpose_values():
        vt_ref[...] = v_ref[...].astype(F32).T.astype(BF16)

    mask_step = [nb + _MOBA_LAG_MASK for nb in range(len(blocks))]
    tile_done = {}
    for nb, (i, _) in enumerate(blocks):
        tile_done[i] = max(tile_done.get(i, 0), mask_step[nb])
    def cast_side():
        side_bf_ref[...] = side_ref[...].astype(BF16)

    events = [(1, 1, transpose_values), (2, 1, cast_side)]
    exp_step = -1
    for nb, (i, _) in enumerate(blocks):
        exp_step = max(exp_step + 1, tile_done[i] + _MOBA_LAG_EXP)
        events += [
            (nb, 0, functools.partial(score_matmul, nb)),
            (exp_step + _MOBA_LAG_PV, 2, functools.partial(value_matmul, nb)),
            (exp_step, 3, functools.partial(exponentiate, nb)),
            (mask_step[nb], 4, functools.partial(mask_and_max, nb)),
        ]
    _emit_in_order(events)


def _moba_attn(qkv, side, *, batch, seq):
    kern = functools.partial(_moba_kernel, seq=seq)
    hq, hk, hv = 0, N_HEADS_MOBA, 2 * N_HEADS_MOBA
    blk = (seq, HEAD_DIM)
    side_in, side_out, side_shape = _side_cast_specs(
        side, batch * N_HEADS_MOBA, lambda b, h: (b * N_HEADS_MOBA + h, 0))
    return pl.pallas_call(
        kern,
        out_shape=(jax.ShapeDtypeStruct((batch * seq, N_HEADS_MOBA * HEAD_DIM), F32), side_shape),
        grid=(batch, N_HEADS_MOBA),
        in_specs=[
            pl.BlockSpec(blk, lambda b, h: (b, hq + h)),
            pl.BlockSpec(blk, lambda b, h: (b, hk + h)),
            pl.BlockSpec(blk, lambda b, h: (b, hv + h)),
            side_in,
        ],
        out_specs=(pl.BlockSpec(blk, lambda b, h: (b, h)), side_out),
        scratch_shapes=[pltpu.VMEM((HEAD_DIM, seq), BF16)],
        compiler_params=pltpu.CompilerParams(
            dimension_semantics=("arbitrary", "arbitrary"),
            vmem_limit_bytes=VMEM_LIMIT_BYTES),
        name="moba_attn",
    )(qkv, qkv, qkv, side)


_SB_LAGS = (1, 1, 2, 1)


def _sb_kernel(q_ref, k_ref, v_ref, side_ref, o_ref, side_bf_ref, vt_ref, *, seq, tile):
    n_tiles = seq // tile
    key_i = lax.broadcasted_iota(jnp.int32, (tile, tile), 0)
    qry_i = lax.broadcasted_iota(jnp.int32, (tile, tile), 1)
    causal = key_i < qry_i
    this_or_later = (qry_i >= key_i).astype(BF16)

    blocks = [(i, j) for i in range(n_tiles - 1, -1, -1) for j in range(i, -1, -1)]
    raw, logit, soft_bf, later_sum, a_bf, acc, carry = {}, {}, {}, {}, {}, {}, {}

    def logit_matmul(nb):
        i, j = blocks[nb]
        raw[nb] = lax.dot_general(k_ref[j * tile:(j + 1) * tile, :], q_ref[i * tile:(i + 1) * tile, :],
                                  _NT, preferred_element_type=F32)

    def softplus(nb):
        i, j = blocks[nb]
        z = raw.pop(nb)
        t = jnp.maximum(z, 0.0) + jnp.log2(1.0 + jnp.exp2(-jnp.abs(z)))
        if j == i:
            t = jnp.where(causal, t, 0.0)
        logit[nb] = z
        soft_bf[nb] = t.astype(BF16)

    def cumsum_matmul(nb):
        later_sum[nb] = jnp.dot(this_or_later, soft_bf.pop(nb), preferred_element_type=F32)

    def weights(nb):
        i, j = blocks[nb]
        inc = later_sum.pop(nb)
        x = logit.pop(nb) - inc
        if j != i:
            x = x - carry[i]
        a = jnp.exp2(x)
        if j == i:
            a = jnp.where(causal, a, 0.0)
        total = inc[0:1, :]
        carry[i] = total if j == i else carry[i] + total
        a_bf[nb] = a.astype(BF16)

    def value_matmul(nb):
        i, j = blocks[nb]
        d = jnp.dot(vt_ref[:, j * tile:(j + 1) * tile], a_bf.pop(nb), preferred_element_type=F32)
        acc[i] = d if j == i else acc[i] + d
        if j == 0:
            o_ref[i * tile:(i + 1) * tile, :] = acc.pop(i).T.astype(o_ref.dtype)

    def transpose_values():
        vt_ref[...] = v_ref[...].astype(F32).T.astype(BF16)

    def cast_side():
        side_bf_ref[...] = side_ref[...].astype(BF16)

    l_soft, l_cum, l_w, l_pv = _SB_LAGS
    events = [(1, 1, transpose_values), (2, 1, cast_side)]
    for nb in range(len(blocks)):
        events += [
            (nb, 0, functools.partial(logit_matmul, nb)),
            (nb + l_soft + l_cum + l_w + l_pv, 2, functools.partial(value_matmul, nb)),
            (nb + l_soft + l_cum, 3, functools.partial(cumsum_matmul, nb)),
            (nb + l_soft + l_cum + l_w, 4, functools.partial(weights, nb)),
            (nb + l_soft, 5, functools.partial(softplus, nb)),
        ]
    _emit_in_order(events)


def _sb_attn(qkv, side, *, batch, seq, tile):
    kern = functools.partial(_sb_kernel, seq=seq, tile=tile)
    base = 3 * N_HEADS_MOBA
    hq, hk, hv = base, base + N_HEADS_SB, base + 2 * N_HEADS_SB
    blk = (seq, HEAD_DIM)
    side_in, side_out, side_shape = _side_cast_specs(
        side, batch * N_HEADS_SB, lambda b, h: (b * N_HEADS_SB + h, 0))
    return pl.pallas_call(
        kern,
        out_shape=(jax.ShapeDtypeStruct((batch * seq, N_HEADS_SB * HEAD_DIM), F32), side_shape),
        grid=(batch, N_HEADS_SB),
        in_specs=[
            pl.BlockSpec(blk, lambda b, h: (b, hq + h)),
            pl.BlockSpec(blk, lambda b, h: (b, hk + h)),
            pl.BlockSpec(blk, lambda b, h: (b, hv + h)),
            side_in,
        ],
        out_specs=(pl.BlockSpec(blk, lambda b, h: (b, h)), side_out),
        scratch_shapes=[pltpu.VMEM((HEAD_DIM, seq), BF16)],
        compiler_params=pltpu.CompilerParams(
            dimension_semantics=("arbitrary", "arbitrary"),
            vmem_limit_bytes=VMEM_LIMIT_BYTES),
        name="sb_attn",
    )(qkv, qkv, qkv, side)


def _out_proj_kernel(oa_ref, ob_ref, ga_ref, gb_ref, w_ref, x_ref, y_ref):
    d_a = oa_ref.shape[1]
    oa = oa_ref[...]
    ob = ob_ref[...]
    na = ((oa * _rms_scale(oa)) * ga_ref[...]).astype(BF16)
    nb = ((ob * _rms_scale(ob)) * gb_ref[...]).astype(BF16)
    y = (jnp.dot(na, w_ref[0:d_a, :], preferred_element_type=F32)
         + jnp.dot(nb, w_ref[d_a:, :], preferred_element_type=F32))
    y_ref[...] = x_ref[...] + y


def _out_proj(o_a, o_b, g_a, g_b, w_bf16, x2, *, tm):
    m, d_a = o_a.shape
    d_b = o_b.shape[1]
    d = w_bf16.shape[1]
    return pl.pallas_call(
        _out_proj_kernel,
        out_shape=jax.ShapeDtypeStruct((m, d), F32),
        grid=(m // tm,),
        in_specs=[
            pl.BlockSpec((tm, d_a), lambda i: (i, 0)),
            pl.BlockSpec((tm, d_b), lambda i: (i, 0)),
            pl.BlockSpec((1, d_a), lambda i: (0, 0)),
            pl.BlockSpec((1, d_b), lambda i: (0, 0)),
            pl.BlockSpec((d_a + d_b, d), lambda i: (0, 0)),
            pl.BlockSpec((tm, d), lambda i: (i, 0)),
        ],
        out_specs=pl.BlockSpec((tm, d), lambda i: (i, 0)),
        compiler_params=pltpu.CompilerParams(
            dimension_semantics=("arbitrary",),
            vmem_limit_bytes=VMEM_LIMIT_BYTES),
        name="out_proj",
    )(o_a, o_b, g_a, g_b, w_bf16, x2)


def _mlp_kernel(x_ref, g_ref, wu_ref, wd_ref, gf_ref, o_ref, h_ref, *, n_chunk):
    f = pl.program_id(1)
    n_f = pl.num_programs(1)

    @pl.when(f == 0)
    def _():
        x = x_ref[...]
        h_ref[...] = ((x * _rms_scale(x)) * g_ref[...]).astype(BF16)
        o_ref[...] = x

    u = jnp.dot(h_ref[...], wu_ref[...], preferred_element_type=F32)
    r = jnp.maximum(u, 0.0)
    act = (r * r).astype(BF16)
    d = o_ref.shape[1]
    cw = d // n_chunk
    for c in range(n_chunk):
        o_ref[:, c * cw:(c + 1) * cw] += jnp.dot(
            act, wd_ref[:, c * cw:(c + 1) * cw], preferred_element_type=F32)

    @pl.when(f == n_f - 1)
    def _():
        y = o_ref[...]
        o_ref[...] = (y * _rms_scale(y)) * gf_ref[...]


def _mlp(x1, g, wu_bf16, wd_bf16, g_final, *, tm, tf):
    m, d = x1.shape
    d_ff = wu_bf16.shape[1]
    kern = functools.partial(_mlp_kernel, n_chunk=4)
    return pl.pallas_call(
        kern,
        out_shape=jax.ShapeDtypeStruct((m, d), F32),
        grid=(m // tm, d_ff // tf),
        in_specs=[
            pl.BlockSpec((tm, d), lambda i, f: (i, 0)),
            pl.BlockSpec((1, d), lambda i, f: (0, 0)),
            pl.BlockSpec((d, tf), lambda i, f: (0, f)),
            pl.BlockSpec((tf, d), lambda i, f: (f, 0)),
            pl.BlockSpec((1, d), lambda i, f: (0, 0)),
        ],
        out_specs=pl.BlockSpec((tm, d), lambda i, f: (i, 0)),
        scratch_shapes=[pltpu.VMEM((tm, d), BF16)],
        compiler_params=pltpu.CompilerParams(
            dimension_semantics=("arbitrary", "arbitrary"),
            vmem_limit_bytes=VMEM_LIMIT_BYTES),
        name="mlp",
    )(x1, g, wu_bf16, wd_bf16, g_final)


def _rope_tables(seq):
    half = ROPE_DIMS // 2
    inv_freq = ROPE_THETA ** (-jnp.arange(half, dtype=F32) / half)
    ang = jnp.arange(seq, dtype=F32)[:, None] * inv_freq[None, :]
    cos, sin = jnp.cos(ang), jnp.sin(ang)
    ones = jnp.ones((seq, HEAD_DIM - ROPE_DIMS), F32)
    zeros_half = jnp.zeros((seq, half), F32)
    zeros_rest = jnp.zeros((seq, HEAD_DIM - ROPE_DIMS), F32)
    cos_t = jnp.concatenate([cos, cos, ones], axis=1)
    sin_lo_t = jnp.concatenate([zeros_half, sin, zeros_rest], axis=1)
    sin_hi_t = jnp.concatenate([-sin, zeros_half, zeros_rest], axis=1)
    return cos_t, sin_lo_t, sin_hi_t


def kernel(x, mix_norm_g, w_in, moba_out_g, sb_out_g, w_out, mlp_norm_g, w_up, w_down, final_norm_g):
    batch, seq, d_model = x.shape
    depth = w_in.shape[0]
    cos_t, sin_lo_t, sin_hi_t = _rope_tables(seq)
    x2 = x.reshape(batch * seq, d_model)
    for l in range(depth):
        qkv, w_out_bf = _qkv_proj(x2, mix_norm_g[l][None, :], w_in[l], cos_t, sin_lo_t, sin_hi_t,
                                  w_out[l], seq=seq, tm=1024, tn=1024)
        o_a, w_up_bf = _moba_attn(qkv, w_up[l], batch=batch, seq=seq)
        o_b, w_down_bf = _sb_attn(qkv, w_down[l], batch=batch, seq=seq, tile=256)
        x1 = _out_proj(o_a, o_b, moba_out_g[l][None, :], sb_out_g[l][None, :],
                       w_out_bf, x2, tm=512)
        last = l == depth - 1
        assert last, "kernel fuses the final RMSNorm into the last layer's MLP; DEPTH must be 1"
        x2 = _mlp(x1, mlp_norm_g[l][None, :], w_up_bf, w_down_bf,
                  final_norm_g[None, :], tm=1024, tf=512)
    return x2.reshape(batch, seq, d_model)
```

```python
import functools
import math

import jax
import jax.numpy as jnp
from jax import lax
from jax.experimental import pallas as pl
from jax.experimental.pallas import tpu as pltpu

HEAD_DIM = 128
N_HEADS_MOBA = 8
N_HEADS_SB = 8
MOBA_BLOCK = 256
MOBA_TOPK = 3
ROPE_THETA = 500000.0
ROPE_DIMS = HEAD_DIM // 4
EPS = 1e-6
NEG = -1e30

F32 = jnp.float32
BF16 = jnp.bfloat16

_NT = (((1,), (1,)), ((), ()))

VMEM_LIMIT_BYTES = 56 * 1024 * 1024
MLP_VMEM_LIMIT_BYTES = 62 * 1024 * 1024

Q_SCALE = HEAD_DIM ** -0.5 * math.log2(math.e)


def _rms_scale(x):
    return lax.rsqrt(jnp.mean(x * x, axis=-1, keepdims=True) + EPS)


def _emit_in_order(events):
    for _, _, thunk in sorted(events, key=lambda e: (e[0], e[1])):
        thunk()


def _qkv_kernel(x_ref, g_ref, w_ref, cos_ref, sin_lo_ref, sin_hi_ref, side_ref, o_ref, side_bf_ref, h_ref):
    j = pl.program_id(1)

    @pl.when(j == 0)
    def _():
        x = x_ref[...]
        h_ref[...] = ((x * _rms_scale(x)) * g_ref[...]).astype(BF16)
        side_bf_ref[...] = side_ref[...].astype(BF16)

    def project():
        return jnp.dot(h_ref[...], w_ref[...].astype(BF16), preferred_element_type=F32)

    def rope(y, post_scale):
        n_heads = y.shape[1] // HEAD_DIM
        half = ROPE_DIMS // 2
        lane = lax.broadcasted_iota(jnp.int32, (y.shape[0], HEAD_DIM), 1)
        partner = jnp.where(lane < ROPE_DIMS, lane ^ half, lane)
        for hd in range(n_heads):
            t = y[:, hd * HEAD_DIM:(hd + 1) * HEAD_DIM]
            swapped = jnp.take_along_axis(t, partner, axis=1)
            r = t * cos_ref[...] + swapped * (sin_lo_ref[...] + sin_hi_ref[...])
            if post_scale is not None:
                r = r * post_scale
            o_ref[:, hd * HEAD_DIM:(hd + 1) * HEAD_DIM] = r.astype(o_ref.dtype)

    @pl.when(j == 0)
    def _():
        rope(project(), Q_SCALE)

    @pl.when(j == 1)
    def _():
        rope(project(), None)

    @pl.when(j == 3)
    def _():
        o_ref[...] = (project() * Q_SCALE).astype(o_ref.dtype)

    @pl.when((j == 2) | (j >= 4))
    def _():
        o_ref[...] = project().astype(o_ref.dtype)


def _side_cast_specs(side, n_steps, index_map):
    rows, cols = side.shape
    slab = rows // n_steps
    assert slab * n_steps == rows and slab % 16 == 0, (rows, n_steps)
    spec = pl.BlockSpec((slab, cols), index_map)
    return spec, spec, jax.ShapeDtypeStruct((rows, cols), BF16)


def _qkv_proj(x2, g, w, cos_t, sin_lo_t, sin_hi_t, side, *, seq, tm, tn):
    m, d = x2.shape
    n = w.shape[1]
    pos_blocks = seq // tm
    tab_spec = pl.BlockSpec((tm, HEAD_DIM), lambda i, j: (i % pos_blocks, 0))
    side_in, side_out, side_shape = _side_cast_specs(side, m // tm, lambda i, j: (i, 0))
    return pl.pallas_call(
        _qkv_kernel,
        out_shape=(jax.ShapeDtypeStruct((m, n), BF16), side_shape),
        grid=(m // tm, n // tn),
        in_specs=[
            pl.BlockSpec((tm, d), lambda i, j: (i, 0)),
            pl.BlockSpec((1, d), lambda i, j: (0, 0)),
            pl.BlockSpec((d, tn), lambda i, j: (0, j)),
            tab_spec, tab_spec, tab_spec,
            side_in,
        ],
        out_specs=(pl.BlockSpec((tm, tn), lambda i, j: (i, j)), side_out),
        scratch_shapes=[pltpu.VMEM((tm, d), BF16)],
        compiler_params=pltpu.CompilerParams(
            dimension_semantics=("arbitrary", "arbitrary"),
            vmem_limit_bytes=VMEM_LIMIT_BYTES),
        name="qkv_proj",
    )(x2, g, w, cos_t, sin_lo_t, sin_hi_t, side)


_MOBA_LAG_MASK, _MOBA_LAG_EXP, _MOBA_LAG_PV = 3, 2, 2


def _moba_kernel(q_ref, k_ref, v_ref, side_ref, o_ref, side_bf_ref, vt_ref, *, seq):
    blk = MOBA_BLOCK
    n_blk = seq // blk

    key_i = lax.broadcasted_iota(jnp.int32, (blk, blk), 0)
    qry_i = lax.broadcasted_iota(jnp.int32, (blk, blk), 1)
    causal = key_i <= qry_i
    blk_id = lax.broadcasted_iota(jnp.int32, (n_blk, blk), 0)

    kf = k_ref[...].astype(F32)
    k_mean = jnp.concatenate(
        [jnp.mean(kf[n * blk:(n + 1) * blk, :], axis=0, keepdims=True) for n in range(n_blk)], axis=0)
    km_hi = k_mean.astype(BF16)
    km_lo = (k_mean - km_hi.astype(F32)).astype(BF16)
    km_rows = jnp.concatenate([km_hi, km_lo], axis=0)

    blocks = [(i, j) for i in range(n_blk - 1, -1, -1) for j in ([i] + list(range(i)))]
    last_of_tile = {i: (i - 1 if i else 0) for i in range(n_blk)}
    raw, gate_raw, bias, scores, col_max, p_bf, acc, row_sum = {}, {}, {}, {}, {}, {}, {}, {}

    def score_matmul(nb):
        i, j = blocks[nb]
        q_i = q_ref[i * blk:(i + 1) * blk, :]
        k_j = k_ref[j * blk:(j + 1) * blk, :]
        if j == i:
            r = lax.dot_general(jnp.concatenate([k_j, km_rows], axis=0), q_i, _NT,
                                preferred_element_type=F32)
            raw[nb] = r[0:blk, :]
            gate_raw[i] = r[blk:blk + n_blk, :] + r[blk + n_blk:blk + 2 * n_blk, :]
        else:
            raw[nb] = lax.dot_general(k_j, q_i, _NT, preferred_element_type=F32)

    def select_blocks(i):
        past = blk_id < i
        g = jnp.where(past, gate_raw.pop(i), NEG)
        rank = jnp.zeros((n_blk, blk), jnp.int32)
        for other in range(n_blk):
            g_o = g[other:other + 1, :]
            beats = (g_o > g) | ((g_o == g) & (other < blk_id))
            rank = rank + beats.astype(jnp.int32)
        bias[i] = jnp.where(past & (rank < MOBA_TOPK), 0.0, NEG).astype(F32)

    def mask_and_max(nb):
        i, j = blocks[nb]
        s = raw.pop(nb)
        if j == i:
            s = jnp.where(causal, s, NEG)
            select_blocks(i)
        else:
            s = s + bias[i][j:j + 1, :]
        scores[nb] = s
        cm = jnp.max(s, axis=0, keepdims=True)
        col_max[i] = cm if j == i else jnp.maximum(col_max[i], cm)

    def exponentiate(nb):
        i, j = blocks[nb]
        p = jnp.exp2(scores.pop(nb) - col_max[i])
        ps = jnp.sum(p, axis=0, keepdims=True)
        row_sum[i] = ps if j == i else row_sum[i] + ps
        p_bf[nb] = p.astype(BF16)

    def value_matmul(nb):
        i, j = blocks[nb]
        d = jnp.dot(vt_ref[:, j * blk:(j + 1) * blk], p_bf.pop(nb), preferred_element_type=F32)
        acc[i] = d if j == i else acc[i] + d
        if j == last_of_tile[i]:
            o_ref[i * blk:(i + 1) * blk, :] = (acc.pop(i) / row_sum.pop(i)).T.astype(o_ref.dtype)

    def transpose_values():
        vt_ref[...] = v_ref[...].astype(F32).T.astype(BF16)

    mask_step = [nb + _MOBA_LAG_MASK for nb in range(len(blocks))]
    tile_done = {}
    for nb, (i, _) in enumerate(blocks):
        tile_done[i] = max(tile_done.get(i, 0), mask_step[nb])
    def cast_side():
        side_bf_ref[...] = side_ref[...].astype(BF16)

    events = [(1, 1, transpose_values), (2, 1, cast_side)]
    exp_step = -1
    for nb, (i, _) in enumerate(blocks):
        exp_step = max(exp_step + 1, tile_done[i] + _MOBA_LAG_EXP)
        events += [
            (nb, 0, functools.partial(score_matmul, nb)),
            (exp_step + _MOBA_LAG_PV, 2, functools.partial(value_matmul, nb)),
            (exp_step, 3, functools.partial(exponentiate, nb)),
            (mask_step[nb], 4, functools.partial(mask_and_max, nb)),
        ]
    _emit_in_order(events)


def _moba_attn(qkv, side, *, batch, seq):
    kern = functools.partial(_moba_kernel, seq=seq)
    hq, hk, hv = 0, N_HEADS_MOBA, 2 * N_HEADS_MOBA
    blk = (seq, HEAD_DIM)
    side_in, side_out, side_shape = _side_cast_specs(
        side, batch * N_HEADS_MOBA, lambda b, h: (b * N_HEADS_MOBA + h, 0))
    return pl.pallas_call(
        kern,
        out_shape=(jax.ShapeDtypeStruct((batch * seq, N_HEADS_MOBA * HEAD_DIM), F32), side_shape),
        grid=(batch, N_HEADS_MOBA),
        in_specs=[
            pl.BlockSpec(blk, lambda b, h: (b, hq + h)),
            pl.BlockSpec(blk, lambda b, h: (b, hk + h)),
            pl.BlockSpec(blk, lambda b, h: (b, hv + h)),
            side_in,
        ],
        out_specs=(pl.BlockSpec(blk, lambda b, h: (b, h)), side_out),
        scratch_shapes=[pltpu.VMEM((HEAD_DIM, seq), BF16)],
        compiler_params=pltpu.CompilerParams(
            dimension_semantics=("arbitrary", "arbitrary"),
            vmem_limit_bytes=VMEM_LIMIT_BYTES),
        name="moba_attn",
    )(qkv, qkv, qkv, side)


_SB_LAGS = (1, 1, 2, 1)


def _sb_kernel(q_ref, k_ref, v_ref, side_ref, o_ref, side_bf_ref, vt_ref, *, seq, tile):
    n_tiles = seq // tile
    key_i = lax.broadcasted_iota(jnp.int32, (tile, tile), 0)
    qry_i = lax.broadcasted_iota(jnp.int32, (tile, tile), 1)
    causal = key_i < qry_i
    this_or_later = (qry_i >= key_i).astype(BF16)

    blocks = [(i, j) for i in range(n_tiles - 1, -1, -1) for j in range(i, -1, -1)]
    raw, logit, soft_bf, later_sum, a_bf, acc, carry = {}, {}, {}, {}, {}, {}, {}

    def logit_matmul(nb):
        i, j = blocks[nb]
        raw[nb] = lax.dot_general(k_ref[j * tile:(j + 1) * tile, :], q_ref[i * tile:(i + 1) * tile, :],
                                  _NT, preferred_element_type=F32)

    def softplus(nb):
        i, j = blocks[nb]
        z = raw.pop(nb)
        t = jnp.maximum(z, 0.0) + jnp.log2(1.0 + jnp.exp2(-jnp.abs(z)))
        if j == i:
            t = jnp.where(causal, t, 0.0)
        logit[nb] = z
        soft_bf[nb] = t.astype(BF16)

    def cumsum_matmul(nb):
        later_sum[nb] = jnp.dot(this_or_later, soft_bf.pop(nb), preferred_element_type=F32)

    def weights(nb):
        i, j = blocks[nb]
        inc = later_sum.pop(nb)
        x = logit.pop(nb) - inc
        if j != i:
            x = x - carry[i]
        a = jnp.exp2(x)
        if j == i:
            a = jnp.where(causal, a, 0.0)
        total = inc[0:1, :]
        carry[i] = total if j == i else carry[i] + total
        a_bf[nb] = a.astype(BF16)

    def value_matmul(nb):
        i, j = blocks[nb]
        d = jnp.dot(vt_ref[:, j * tile:(j + 1) * tile], a_bf.pop(nb), preferred_element_type=F32)
        acc[i] = d if j == i else acc[i] + d
        if j == 0:
            o_ref[i * tile:(i + 1) * tile, :] = acc.pop(i).T.astype(o_ref.dtype)

    def transpose_values():
        vt_ref[...] = v_ref[...].astype(F32).T.astype(BF16)

    def cast_side():
        side_bf_ref[...] = side_ref[...].astype(BF16)

    l_soft, l_cum, l_w, l_pv = _SB_LAGS
    events = [(1, 1, transpose_values), (2, 1, cast_side)]
    for nb in range(len(blocks)):
        events += [
            (nb, 0, functools.partial(logit_matmul, nb)),
            (nb + l_soft + l_cum + l_w + l_pv, 2, functools.partial(value_matmul, nb)),
            (nb + l_soft + l_cum, 3, functools.partial(cumsum_matmul, nb)),
            (nb + l_soft + l_cum + l_w, 4, functools.partial(weights, nb)),
            (nb + l_soft, 5, functools.partial(softplus, nb)),
        ]
    _emit_in_order(events)


def _sb_attn(qkv, side, *, batch, seq, tile):
    kern = functools.partial(_sb_kernel, seq=seq, tile=tile)
    base = 3 * N_HEADS_MOBA
    hq, hk, hv = base, base + N_HEADS_SB, base + 2 * N_HEADS_SB
    blk = (seq, HEAD_DIM)
    side_in, side_out, side_shape = _side_cast_specs(
        side, batch * N_HEADS_SB, lambda b, h: (b * N_HEADS_SB + h, 0))
    return pl.pallas_call(
        kern,
        out_shape=(jax.ShapeDtypeStruct((batch * seq, N_HEADS_SB * HEAD_DIM), F32), side_shape),
        grid=(batch, N_HEADS_SB),
        in_specs=[
            pl.BlockSpec(blk, lambda b, h: (b, hq + h)),
            pl.BlockSpec(blk, lambda b, h: (b, hk + h)),
            pl.BlockSpec(blk, lambda b, h: (b, hv + h)),
            side_in,
        ],
        out_specs=(pl.BlockSpec(blk, lambda b, h: (b, h)), side_out),
        scratch_shapes=[pltpu.VMEM((HEAD_DIM, seq), BF16)],
        compiler_params=pltpu.CompilerParams(
            dimension_semantics=("arbitrary", "arbitrary"),
            vmem_limit_bytes=VMEM_LIMIT_BYTES),
        name="sb_attn",
    )(qkv, qkv, qkv, side)


def _out_proj_kernel(oa_ref, ob_ref, ga_ref, gb_ref, w_ref, x_ref, y_ref):
    d_a = oa_ref.shape[1]
    oa = oa_ref[...]
    ob = ob_ref[...]
    na = ((oa * _rms_scale(oa)) * ga_ref[...]).astype(BF16)
    nb = ((ob * _rms_scale(ob)) * gb_ref[...]).astype(BF16)
    y = (jnp.dot(na, w_ref[0:d_a, :], preferred_element_type=F32)
         + jnp.dot(nb, w_ref[d_a:, :], preferred_element_type=F32))
    y_ref[...] = x_ref[...] + y


def _out_proj(o_a, o_b, g_a, g_b, w_bf16, x2, *, tm):
    m, d_a = o_a.shape
    d_b = o_b.shape[1]
    d = w_bf16.shape[1]
    return pl.pallas_call(
        _out_proj_kernel,
        out_shape=jax.ShapeDtypeStruct((m, d), F32),
        grid=(m // tm,),
        in_specs=[
            pl.BlockSpec((tm, d_a), lambda i: (i, 0)),
            pl.BlockSpec((tm, d_b), lambda i: (i, 0)),
            pl.BlockSpec((1, d_a), lambda i: (0, 0)),
            pl.BlockSpec((1, d_b), lambda i: (0, 0)),
            pl.BlockSpec((d_a + d_b, d), lambda i: (0, 0)),
            pl.BlockSpec((tm, d), lambda i: (i, 0)),
        ],
        out_specs=pl.BlockSpec((tm, d), lambda i: (i, 0)),
        compiler_params=pltpu.CompilerParams(
            dimension_semantics=("arbitrary",),
            vmem_limit_bytes=VMEM_LIMIT_BYTES),
        name="out_proj",
    )(o_a, o_b, g_a, g_b, w_bf16, x2)


def _mlp_kernel(x_ref, g_ref, wu_ref, wd_ref, gf_ref, o_ref, h_ref, *, n_chunk):
    f = pl.program_id(1)
    n_f = pl.num_programs(1)

    @pl.when(f == 0)
    def _():
        x = x_ref[...]
        h_ref[...] = ((x * _rms_scale(x)) * g_ref[...]).astype(BF16)
        o_ref[...] = x

    u = jnp.dot(h_ref[...], wu_ref[...], preferred_element_type=F32)
    r = jnp.maximum(u, 0.0)
    act = (r * r).astype(BF16)
    d = o_ref.shape[1]
    cw = d // n_chunk
    for c in range(n_chunk):
        o_ref[:, c * cw:(c + 1) * cw] += jnp.dot(
            act, wd_ref[:, c * cw:(c + 1) * cw], preferred_element_type=F32)

    @pl.when(f == n_f - 1)
    def _():
        y = o_ref[...]
        o_ref[...] = (y * _rms_scale(y)) * gf_ref[...]


def _mlp(x1, g, wu_bf16, wd_bf16, g_final, *, tm, tf):
    m, d = x1.shape
    d_ff = wu_bf16.shape[1]
    kern = functools.partial(_mlp_kernel, n_chunk=4)
    return pl.pallas_call(
        kern,
        out_shape=jax.ShapeDtypeStruct((m, d), F32),
        grid=(m // tm, d_ff // tf),
        in_specs=[
            pl.BlockSpec((tm, d), lambda i, f: (i, 0)),
            pl.BlockSpec((1, d), lambda i, f: (0, 0)),
            pl.BlockSpec((d, tf), lambda i, f: (0, f)),
            pl.BlockSpec((tf, d), lambda i, f: (f, 0)),
            pl.BlockSpec((1, d), lambda i, f: (0, 0)),
        ],
        out_specs=pl.BlockSpec((tm, d), lambda i, f: (i, 0)),
        scratch_shapes=[pltpu.VMEM((tm, d), BF16)],
        compiler_params=pltpu.CompilerParams(
            dimension_semantics=("arbitrary", "arbitrary"),
            vmem_limit_bytes=MLP_VMEM_LIMIT_BYTES),
        name="mlp",
    )(x1, g, wu_bf16, wd_bf16, g_final)


def _rope_tables(seq):
    half = ROPE_DIMS // 2
    inv_freq = ROPE_THETA ** (-jnp.arange(half, dtype=F32) / half)
    ang = jnp.arange(seq, dtype=F32)[:, None] * inv_freq[None, :]
    cos, sin = jnp.cos(ang), jnp.sin(ang)
    ones = jnp.ones((seq, HEAD_DIM - ROPE_DIMS), F32)
    zeros_half = jnp.zeros((seq, half), F32)
    zeros_rest = jnp.zeros((seq, HEAD_DIM - ROPE_DIMS), F32)
    cos_t = jnp.concatenate([cos, cos, ones], axis=1)
    sin_lo_t = jnp.concatenate([zeros_half, sin, zeros_rest], axis=1)
    sin_hi_t = jnp.concatenate([-sin, zeros_half, zeros_rest], axis=1)
    return cos_t, sin_lo_t, sin_hi_t


def kernel(x, mix_norm_g, w_in, moba_out_g, sb_out_g, w_out, mlp_norm_g, w_up, w_down, final_norm_g):
    batch, seq, d_model = x.shape
    depth = w_in.shape[0]
    cos_t, sin_lo_t, sin_hi_t = _rope_tables(seq)
    x2 = x.reshape(batch * seq, d_model)
    for l in range(depth):
        qkv, w_out_bf = _qkv_proj(x2, mix_norm_g[l][None, :], w_in[l], cos_t, sin_lo_t, sin_hi_t,
                                  w_out[l], seq=seq, tm=1024, tn=1024)
        o_a, w_up_bf = _moba_attn(qkv, w_up[l], batch=batch, seq=seq)
        o_b, w_down_bf = _sb_attn(qkv, w_down[l], batch=batch, seq=seq, tile=256)
        x1 = _out_proj(o_a, o_b, moba_out_g[l][None, :], sb_out_g[l][None, :],
                       w_out_bf, x2, tm=512)
        last = l == depth - 1
        assert last, "kernel fuses the final RMSNorm into the last layer's MLP; DEPTH must be 1"
        x2 = _mlp(x1, mlp_norm_g[l][None, :], w_up_bf, w_down_bf,
                  final_norm_g[None, :], tm=1024, tf=1024)
    return x2.reshape(batch, seq, d_model)
```

```python
import functools
import math

import jax
import jax.numpy as jnp
from jax import lax
from jax.experimental import pallas as pl
from jax.experimental.pallas import tpu as pltpu

HEAD_DIM = 128
N_HEADS_MOBA = 8
N_HEADS_SB = 8
MOBA_BLOCK = 256
MOBA_TOPK = 3
ROPE_THETA = 500000.0
ROPE_DIMS = HEAD_DIM // 4
EPS = 1e-6
NEG = -1e30

F32 = jnp.float32
BF16 = jnp.bfloat16

_NT = (((1,), (1,)), ((), ()))

VMEM_LIMIT_BYTES = 56 * 1024 * 1024
LARGE_VMEM_LIMIT_BYTES = 62 * 1024 * 1024

Q_SCALE = HEAD_DIM ** -0.5 * math.log2(math.e)


def _rms_scale(x):
    return lax.rsqrt(jnp.mean(x * x, axis=-1, keepdims=True) + EPS)


def _emit_in_order(events):
    for _, _, thunk in sorted(events, key=lambda e: (e[0], e[1])):
        thunk()


def _qkv_kernel(x_ref, g_ref, w_ref, cos_ref, sin_ref, o_ref, h_ref):
    j = pl.program_id(1)
    i = pl.program_id(2)
    tm = x_ref.shape[0]
    rows = pl.ds(pl.multiple_of(i * tm, tm), tm)

    @pl.when(j == 0)
    def _():
        x = x_ref[...]
        h_ref[rows, :] = ((x * _rms_scale(x)) * g_ref[...]).astype(BF16)

    def project():
        return jnp.dot(h_ref[rows, :], w_ref[...].astype(BF16), preferred_element_type=F32)

    def rope(y, post_scale):
        n_heads = y.shape[1] // HEAD_DIM
        half = ROPE_DIMS // 2
        lane = lax.broadcasted_iota(jnp.int32, (y.shape[0], HEAD_DIM), 1)
        partner = jnp.where(lane < ROPE_DIMS, lane ^ half, lane)
        for hd in range(n_heads):
            t = y[:, hd * HEAD_DIM:(hd + 1) * HEAD_DIM]
            swapped = jnp.take_along_axis(t, partner, axis=1)
            r = t * cos_ref[...] + swapped * sin_ref[...]
            if post_scale is not None:
                r = r * post_scale
            o_ref[:, hd * HEAD_DIM:(hd + 1) * HEAD_DIM] = r.astype(o_ref.dtype)

    @pl.when(j == 0)
    def _():
        rope(project(), Q_SCALE)

    @pl.when(j == 1)
    def _():
        rope(project(), None)

    @pl.when(j == 3)
    def _():
        o_ref[...] = (project() * Q_SCALE).astype(o_ref.dtype)

    @pl.when((j == 2) | (j >= 4))
    def _():
        o_ref[...] = project().astype(o_ref.dtype)


def _side_cast_specs(side, n_steps, index_map):
    rows, cols = side.shape
    slab = rows // n_steps
    assert slab * n_steps == rows and slab % 16 == 0, (rows, n_steps)
    spec = pl.BlockSpec((slab, cols), index_map)
    return spec, spec, jax.ShapeDtypeStruct((rows, cols), BF16)


def _qkv_proj(x2, g, w, cos_t, sin_t, *, seq, tm, tn, n_groups):
    m, d = x2.shape
    n = w.shape[1]
    tiles = m // (tm * n_groups)
    pos_blocks = seq // tm
    n_rope_tiles = 2

    def x_tile(gr, j, i):
        return gr * tiles + jnp.where(j == 0, i, tiles - 1)

    tab_spec = pl.BlockSpec(
        (tm, HEAD_DIM), lambda gr, j, i: (jnp.where(j < n_rope_tiles, (gr * tiles + i) % pos_blocks, 0), 0))
    return pl.pallas_call(
        _qkv_kernel,
        out_shape=jax.ShapeDtypeStruct((m, n), BF16),
        grid=(n_groups, n // tn, tiles),
        in_specs=[
            pl.BlockSpec((tm, d), lambda gr, j, i: (x_tile(gr, j, i), 0)),
            pl.BlockSpec((1, d), lambda gr, j, i: (0, 0)),
            pl.BlockSpec((d, tn), lambda gr, j, i: (0, j)),
            tab_spec, tab_spec,
        ],
        out_specs=pl.BlockSpec((tm, tn), lambda gr, j, i: (gr * tiles + i, j)),
        scratch_shapes=[pltpu.VMEM((tiles * tm, d), BF16)],
        compiler_params=pltpu.CompilerParams(
            dimension_semantics=("arbitrary", "arbitrary", "arbitrary"),
            vmem_limit_bytes=LARGE_VMEM_LIMIT_BYTES),
        name="qkv_proj",
    )(x2, g, w, cos_t, sin_t)


_MOBA_LAG_MASK, _MOBA_LAG_EXP, _MOBA_LAG_PV = 3, 2, 2


def _moba_kernel(q_ref, k_ref, v_ref, side_ref, side2_ref, o_ref, side_bf_ref, side2_bf_ref, vt_ref, *, seq):
    blk = MOBA_BLOCK
    n_blk = seq // blk

    key_i = lax.broadcasted_iota(jnp.int32, (blk, blk), 0)
    qry_i = lax.broadcasted_iota(jnp.int32, (blk, blk), 1)
    causal = key_i <= qry_i
    blk_id = lax.broadcasted_iota(jnp.int32, (n_blk, blk), 0)

    kf = k_ref[...].astype(F32)
    k_mean = jnp.concatenate(
        [jnp.mean(kf[n * blk:(n + 1) * blk, :], axis=0, keepdims=True) for n in range(n_blk)], axis=0)
    km_hi = k_mean.astype(BF16)
    km_lo = (k_mean - km_hi.astype(F32)).astype(BF16)
    km_rows = jnp.concatenate([km_hi, km_lo], axis=0)

    blocks = [(i, j) for i in range(n_blk - 1, -1, -1) for j in ([i] + list(range(i)))]
    last_of_tile = {i: (i - 1 if i else 0) for i in range(n_blk)}
    raw, gate_raw, bias, scores, col_max, p_bf, acc, row_sum = {}, {}, {}, {}, {}, {}, {}, {}

    def score_matmul(nb):
        i, j = blocks[nb]
        q_i = q_ref[i * blk:(i + 1) * blk, :]
        k_j = k_ref[j * blk:(j + 1) * blk, :]
        if j == i:
            r = lax.dot_general(jnp.concatenate([k_j, km_rows], axis=0), q_i, _NT,
                                preferred_element_type=F32)
            raw[nb] = r[0:blk, :]
            gate_raw[i] = r[blk:blk + n_blk, :] + r[blk + n_blk:blk + 2 * n_blk, :]
        else:
            raw[nb] = lax.dot_general(k_j, q_i, _NT, preferred_element_type=F32)

    def select_blocks(i):
        past = blk_id < i
        g = jnp.where(past, gate_raw.pop(i), NEG)
        rank = jnp.zeros((n_blk, blk), jnp.int32)
        for other in range(n_blk):
            g_o = g[other:other + 1, :]
            beats = (g_o > g) | ((g_o == g) & (other < blk_id))
            rank = rank + beats.astype(jnp.int32)
        bias[i] = jnp.where(past & (rank < MOBA_TOPK), 0.0, NEG).astype(F32)

    def mask_and_max(nb):
        i, j = blocks[nb]
        s = raw.pop(nb)
        if j == i:
            s = jnp.where(causal, s, NEG)
            select_blocks(i)
        else:
            s = s + bias[i][j:j + 1, :]
        scores[nb] = s
        cm = jnp.max(s, axis=0, keepdims=True)
        col_max[i] = cm if j == i else jnp.maximum(col_max[i], cm)

    def exponentiate(nb):
        i, j = blocks[nb]
        p = jnp.exp2(scores.pop(nb) - col_max[i])
        ps = jnp.sum(p, axis=0, keepdims=True)
        row_sum[i] = ps if j == i else row_sum[i] + ps
        p_bf[nb] = p.astype(BF16)

    def value_matmul(nb):
        i, j = blocks[nb]
        d = jnp.dot(vt_ref[:, j * blk:(j + 1) * blk], p_bf.pop(nb), preferred_element_type=F32)
        acc[i] = d if j == i else acc[i] + d
        if j == last_of_tile[i]:
            o_ref[i * blk:(i + 1) * blk, :] = (acc.pop(i) / row_sum.pop(i)).T.astype(o_ref.dtype)

    def transpose_values():
        vt_ref[...] = v_ref[...].astype(F32).T.astype(BF16)

    mask_step = [nb + _MOBA_LAG_MASK for nb in range(len(blocks))]
    tile_done = {}
    for nb, (i, _) in enumerate(blocks):
        tile_done[i] = max(tile_done.get(i, 0), mask_step[nb])
    def cast_side():
        side_bf_ref[...] = side_ref[...].astype(BF16)
        side2_bf_ref[...] = side2_ref[...].astype(BF16)

    events = [(1, 1, transpose_values), (2, 1, cast_side)]
    exp_step = -1
    for nb, (i, _) in enumerate(blocks):
        exp_step = max(exp_step + 1, tile_done[i] + _MOBA_LAG_EXP)
        events += [
            (nb, 0, functools.partial(score_matmul, nb)),
            (exp_step + _MOBA_LAG_PV, 2, functools.partial(value_matmul, nb)),
            (exp_step, 3, functools.partial(exponentiate, nb)),
            (mask_step[nb], 4, functools.partial(mask_and_max, nb)),
        ]
    _emit_in_order(events)


def _moba_attn(qkv, side, side2, *, batch, seq):
    kern = functools.partial(_moba_kernel, seq=seq)
    hq, hk, hv = 0, N_HEADS_MOBA, 2 * N_HEADS_MOBA
    blk = (seq, HEAD_DIM)
    slab_of_step = lambda b, h: (b * N_HEADS_MOBA + h, 0)
    side_in, side_out, side_shape = _side_cast_specs(side, batch * N_HEADS_MOBA, slab_of_step)
    side2_in, side2_out, side2_shape = _side_cast_specs(side2, batch * N_HEADS_MOBA, slab_of_step)
    return pl.pallas_call(
        kern,
        out_shape=(jax.ShapeDtypeStruct((batch * seq, N_HEADS_MOBA * HEAD_DIM), F32),
                   side_shape, side2_shape),
        grid=(batch, N_HEADS_MOBA),
        in_specs=[
            pl.BlockSpec(blk, lambda b, h: (b, hq + h)),
            pl.BlockSpec(blk, lambda b, h: (b, hk + h)),
            pl.BlockSpec(blk, lambda b, h: (b, hv + h)),
            side_in, side2_in,
        ],
        out_specs=(pl.BlockSpec(blk, lambda b, h: (b, h)), side_out, side2_out),
        scratch_shapes=[pltpu.VMEM((HEAD_DIM, seq), BF16)],
        compiler_params=pltpu.CompilerParams(
            dimension_semantics=("arbitrary", "arbitrary"),
            vmem_limit_bytes=VMEM_LIMIT_BYTES),
        name="moba_attn",
    )(qkv, qkv, qkv, side, side2)


_SB_LAGS = (1, 1, 2, 1)


def _sb_kernel(q_ref, k_ref, v_ref, side_ref, o_ref, side_bf_ref, vt_ref, *, seq, tile):
    n_tiles = seq // tile
    key_i = lax.broadcasted_iota(jnp.int32, (tile, tile), 0)
    qry_i = lax.broadcasted_iota(jnp.int32, (tile, tile), 1)
    causal = key_i < qry_i
    this_or_later = (qry_i >= key_i).astype(BF16)

    blocks = [(i, j) for i in range(n_tiles - 1, -1, -1) for j in range(i, -1, -1)]
    raw, logit, soft_bf, later_sum, a_bf, acc, carry = {}, {}, {}, {}, {}, {}, {}

    def logit_matmul(nb):
        i, j = blocks[nb]
        raw[nb] = lax.dot_general(k_ref[j * tile:(j + 1) * tile, :], q_ref[i * tile:(i + 1) * tile, :],
                                  _NT, preferred_element_type=F32)

    def softplus(nb):
        i, j = blocks[nb]
        z = raw.pop(nb)
        t = jnp.maximum(z, 0.0) + jnp.log2(1.0 + jnp.exp2(-jnp.abs(z)))
        if j == i:
            t = jnp.where(causal, t, 0.0)
        logit[nb] = z
        soft_bf[nb] = t.astype(BF16)

    def cumsum_matmul(nb):
        later_sum[nb] = jnp.dot(this_or_later, soft_bf.pop(nb), preferred_element_type=F32)

    def weights(nb):
        i, j = blocks[nb]
        inc = later_sum.pop(nb)
        x = logit.pop(nb) - inc
        if j != i:
            x = x - carry[i]
        a = jnp.exp2(x)
        if j == i:
            a = jnp.where(causal, a, 0.0)
        total = inc[0:1, :]
        carry[i] = total if j == i else carry[i] + total
        a_bf[nb] = a.astype(BF16)

    def value_matmul(nb):
        i, j = blocks[nb]
        d = jnp.dot(vt_ref[:, j * tile:(j + 1) * tile], a_bf.pop(nb), preferred_element_type=F32)
        acc[i] = d if j == i else acc[i] + d
        if j == 0:
            o_ref[i * tile:(i + 1) * tile, :] = acc.pop(i).T.astype(o_ref.dtype)

    def transpose_values():
        vt_ref[...] = v_ref[...].astype(F32).T.astype(BF16)

    def cast_side():
        side_bf_ref[...] = side_ref[...].astype(BF16)

    l_soft, l_cum, l_w, l_pv = _SB_LAGS
    events = [(1, 1, transpose_values), (2, 1, cast_side)]
    for nb in range(len(blocks)):
        events += [
            (nb, 0, functools.partial(logit_matmul, nb)),
            (nb + l_soft + l_cum + l_w + l_pv, 2, functools.partial(value_matmul, nb)),
            (nb + l_soft + l_cum, 3, functools.partial(cumsum_matmul, nb)),
            (nb + l_soft + l_cum + l_w, 4, functools.partial(weights, nb)),
            (nb + l_soft, 5, functools.partial(softplus, nb)),
        ]
    _emit_in_order(events)


def _sb_attn(qkv, side, *, batch, seq, tile):
    kern = functools.partial(_sb_kernel, seq=seq, tile=tile)
    base = 3 * N_HEADS_MOBA
    hq, hk, hv = base, base + N_HEADS_SB, base + 2 * N_HEADS_SB
    blk = (seq, HEAD_DIM)
    side_in, side_out, side_shape = _side_cast_specs(
        side, batch * N_HEADS_SB, lambda b, h: (b * N_HEADS_SB + h, 0))
    return pl.pallas_call(
        kern,
        out_shape=(jax.ShapeDtypeStruct((batch * seq, N_HEADS_SB * HEAD_DIM), F32), side_shape),
        grid=(batch, N_HEADS_SB),
        in_specs=[
            pl.BlockSpec(blk, lambda b, h: (b, hq + h)),
            pl.BlockSpec(blk, lambda b, h: (b, hk + h)),
            pl.BlockSpec(blk, lambda b, h: (b, hv + h)),
            side_in,
        ],
        out_specs=(pl.BlockSpec(blk, lambda b, h: (b, h)), side_out),
        scratch_shapes=[pltpu.VMEM((HEAD_DIM, seq), BF16)],
        compiler_params=pltpu.CompilerParams(
            dimension_semantics=("arbitrary", "arbitrary"),
            vmem_limit_bytes=VMEM_LIMIT_BYTES),
        name="sb_attn",
    )(qkv, qkv, qkv, side)


def _out_proj_kernel(oa_ref, ob_ref, ga_ref, gb_ref, w_ref, x_ref, y_ref):
    d_a = oa_ref.shape[1]
    oa = oa_ref[...]
    ob = ob_ref[...]
    na = ((oa * _rms_scale(oa)) * ga_ref[...]).astype(BF16)
    nb = ((ob * _rms_scale(ob)) * gb_ref[...]).astype(BF16)
    y = (jnp.dot(na, w_ref[0:d_a, :], preferred_element_type=F32)
         + jnp.dot(nb, w_ref[d_a:, :], preferred_element_type=F32))
    y_ref[...] = x_ref[...] + y


def _out_proj(o_a, o_b, g_a, g_b, w_bf16, x2, *, tm):
    m, d_a = o_a.shape
    d_b = o_b.shape[1]
    d = w_bf16.shape[1]
    return pl.pallas_call(
        _out_proj_kernel,
        out_shape=jax.ShapeDtypeStruct((m, d), F32),
        grid=(m // tm,),
        in_specs=[
            pl.BlockSpec((tm, d_a), lambda i: (i, 0)),
            pl.BlockSpec((tm, d_b), lambda i: (i, 0)),
            pl.BlockSpec((1, d_a), lambda i: (0, 0)),
            pl.BlockSpec((1, d_b), lambda i: (0, 0)),
            pl.BlockSpec((d_a + d_b, d), lambda i: (0, 0)),
            pl.BlockSpec((tm, d), lambda i: (i, 0)),
        ],
        out_specs=pl.BlockSpec((tm, d), lambda i: (i, 0)),
        compiler_params=pltpu.CompilerParams(
            dimension_semantics=("arbitrary",),
            vmem_limit_bytes=VMEM_LIMIT_BYTES),
        name="out_proj",
    )(o_a, o_b, g_a, g_b, w_bf16, x2)


def _mlp_kernel(x_ref, g_ref, wu_ref, wd_ref, gf_ref, o_ref, h_ref, *, n_chunk):
    f = pl.program_id(1)
    n_f = pl.num_programs(1)

    @pl.when(f == 0)
    def _():
        x = x_ref[...]
        h_ref[...] = ((x * _rms_scale(x)) * g_ref[...]).astype(BF16)
        o_ref[...] = x

    u = jnp.dot(h_ref[...], wu_ref[...], preferred_element_type=F32)
    r = jnp.maximum(u, 0.0)
    act = (r * r).astype(BF16)
    d = o_ref.shape[1]
    cw = d // n_chunk
    for c in range(n_chunk):
        o_ref[:, c * cw:(c + 1) * cw] += jnp.dot(
            act, wd_ref[:, c * cw:(c + 1) * cw], preferred_element_type=F32)

    @pl.when(f == n_f - 1)
    def _():
        y = o_ref[...]
        o_ref[...] = (y * _rms_scale(y)) * gf_ref[...]


def _mlp(x1, g, wu_bf16, wd_bf16, g_final, *, tm, tf):
    m, d = x1.shape
    d_ff = wu_bf16.shape[1]
    kern = functools.partial(_mlp_kernel, n_chunk=4)
    return pl.pallas_call(
        kern,
        out_shape=jax.ShapeDtypeStruct((m, d), F32),
        grid=(m // tm, d_ff // tf),
        in_specs=[
            pl.BlockSpec((tm, d), lambda i, f: (i, 0)),
            pl.BlockSpec((1, d), lambda i, f: (0, 0)),
            pl.BlockSpec((d, tf), lambda i, f: (0, f)),
            pl.BlockSpec((tf, d), lambda i, f: (f, 0)),
            pl.BlockSpec((1, d), lambda i, f: (0, 0)),
        ],
        out_specs=pl.BlockSpec((tm, d), lambda i, f: (i, 0)),
        scratch_shapes=[pltpu.VMEM((tm, d), BF16)],
        compiler_params=pltpu.CompilerParams(
            dimension_semantics=("arbitrary", "arbitrary"),
            vmem_limit_bytes=LARGE_VMEM_LIMIT_BYTES),
        name="mlp",
    )(x1, g, wu_bf16, wd_bf16, g_final)


def _rope_tables(seq):
    half = ROPE_DIMS // 2
    inv_freq = ROPE_THETA ** (-jnp.arange(half, dtype=F32) / half)
    ang = jnp.arange(seq, dtype=F32)[:, None] * inv_freq[None, :]
    cos, sin = jnp.cos(ang), jnp.sin(ang)
    ones = jnp.ones((seq, HEAD_DIM - ROPE_DIMS), F32)
    zeros_rest = jnp.zeros((seq, HEAD_DIM - ROPE_DIMS), F32)
    cos_t = jnp.concatenate([cos, cos, ones], axis=1)
    sin_t = jnp.concatenate([-sin, sin, zeros_rest], axis=1)
    return cos_t, sin_t


def kernel(x, mix_norm_g, w_in, moba_out_g, sb_out_g, w_out, mlp_norm_g, w_up, w_down, final_norm_g):
    batch, seq, d_model = x.shape
    depth = w_in.shape[0]
    cos_t, sin_t = _rope_tables(seq)
    x2 = x.reshape(batch * seq, d_model)
    for l in range(depth):
        qkv = _qkv_proj(x2, mix_norm_g[l][None, :], w_in[l], cos_t, sin_t,
                        seq=seq, tm=1024, tn=1024, n_groups=2)
        o_a, w_up_bf, w_out_bf = _moba_attn(qkv, w_up[l], w_out[l], batch=batch, seq=seq)
        o_b, w_down_bf = _sb_attn(qkv, w_down[l], batch=batch, seq=seq, tile=256)
        x1 = _out_proj(o_a, o_b, moba_out_g[l][None, :], sb_out_g[l][None, :],
                       w_out_bf, x2, tm=512)
        last = l == depth - 1
        assert last, "kernel fuses the final RMSNorm into the last layer's MLP; DEPTH must be 1"
        x2 = _mlp(x1, mlp_norm_g[l][None, :], w_up_bf, w_down_bf,
                  final_norm_g[None, :], tm=1024, tf=1024)
    return x2.reshape(batch, seq, d_model)
```

```python
import functools
import math

import jax
import jax.numpy as jnp
from jax import lax
from jax.experimental import pallas as pl
from jax.experimental.pallas import tpu as pltpu

HEAD_DIM = 128
N_HEADS_MOBA = 8
N_HEADS_SB = 8
MOBA_BLOCK = 256
MOBA_TOPK = 3
ROPE_THETA = 500000.0
ROPE_DIMS = HEAD_DIM // 4
EPS = 1e-6
NEG = -1e30

F32 = jnp.float32
BF16 = jnp.bfloat16
ATTN_OUT_DTYPE = BF16
ATTN_HEADS_PER_STEP = 2

_NT = (((1,), (1,)), ((), ()))

VMEM_LIMIT_BYTES = 56 * 1024 * 1024
LARGE_VMEM_LIMIT_BYTES = 62 * 1024 * 1024

Q_SCALE = HEAD_DIM ** -0.5 * math.log2(math.e)


def _rms_scale(x):
    return lax.rsqrt(jnp.mean(x * x, axis=-1, keepdims=True) + EPS)


def _emit_in_order(events):
    for _, _, thunk in sorted(events, key=lambda e: (e[0], e[1])):
        thunk()


def _qkv_kernel(x_ref, g_ref, w_ref, cos_ref, sin_ref, o_ref, h_ref):
    j = pl.program_id(1)
    i = pl.program_id(2)
    tm = x_ref.shape[0]
    rows = pl.ds(pl.multiple_of(i * tm, tm), tm)

    @pl.when(j == 0)
    def _():
        x = x_ref[...]
        h_ref[rows, :] = ((x * _rms_scale(x)) * g_ref[...]).astype(BF16)

    def project():
        return jnp.dot(h_ref[rows, :], w_ref[...].astype(BF16), preferred_element_type=F32)

    def rope(y, post_scale):
        n_heads = y.shape[1] // HEAD_DIM
        half = ROPE_DIMS // 2
        lane = lax.broadcasted_iota(jnp.int32, (y.shape[0], HEAD_DIM), 1)
        partner = jnp.where(lane < ROPE_DIMS, lane ^ half, lane)
        for hd in range(n_heads):
            t = y[:, hd * HEAD_DIM:(hd + 1) * HEAD_DIM]
            swapped = jnp.take_along_axis(t, partner, axis=1)
            r = t * cos_ref[...] + swapped * sin_ref[...]
            if post_scale is not None:
                r = r * post_scale
            o_ref[:, hd * HEAD_DIM:(hd + 1) * HEAD_DIM] = r.astype(o_ref.dtype)

    @pl.when(j == 0)
    def _():
        rope(project(), Q_SCALE)

    @pl.when(j == 1)
    def _():
        rope(project(), None)

    @pl.when(j == 3)
    def _():
        o_ref[...] = (project() * Q_SCALE).astype(o_ref.dtype)

    @pl.when((j == 2) | (j >= 4))
    def _():
        o_ref[...] = project().astype(o_ref.dtype)


def _side_cast_specs(side, n_steps, index_map):
    rows, cols = side.shape
    slab = rows // n_steps
    assert slab * n_steps == rows and slab % 16 == 0, (rows, n_steps)
    spec = pl.BlockSpec((slab, cols), index_map)
    return spec, spec, jax.ShapeDtypeStruct((rows, cols), BF16)


def _qkv_proj(x2, g, w, cos_t, sin_t, *, seq, tm, tn, n_groups):
    m, d = x2.shape
    n = w.shape[1]
    tiles = m // (tm * n_groups)
    pos_blocks = seq // tm
    n_rope_tiles = 2

    def x_tile(gr, j, i):
        return gr * tiles + jnp.where(j == 0, i, tiles - 1)

    tab_spec = pl.BlockSpec(
        (tm, HEAD_DIM), lambda gr, j, i: (jnp.where(j < n_rope_tiles, (gr * tiles + i) % pos_blocks, 0), 0))
    return pl.pallas_call(
        _qkv_kernel,
        out_shape=jax.ShapeDtypeStruct((m, n), BF16),
        grid=(n_groups, n // tn, tiles),
        in_specs=[
            pl.BlockSpec((tm, d), lambda gr, j, i: (x_tile(gr, j, i), 0)),
            pl.BlockSpec((1, d), lambda gr, j, i: (0, 0)),
            pl.BlockSpec((d, tn), lambda gr, j, i: (0, j)),
            tab_spec, tab_spec,
        ],
        out_specs=pl.BlockSpec((tm, tn), lambda gr, j, i: (gr * tiles + i, j)),
        scratch_shapes=[pltpu.VMEM((tiles * tm, d), BF16)],
        compiler_params=pltpu.CompilerParams(
            dimension_semantics=("arbitrary", "arbitrary", "arbitrary"),
            vmem_limit_bytes=LARGE_VMEM_LIMIT_BYTES),
        name="qkv_proj",
    )(x2, g, w, cos_t, sin_t)


_MOBA_LAG_MASK, _MOBA_LAG_EXP, _MOBA_LAG_PV = 3, 2, 2


def _moba_kernel(q_ref, k_ref, v_ref, side_ref, side2_ref, o_ref, side_bf_ref, side2_bf_ref, vt_ref,
                 *, seq, heads):
    blk = MOBA_BLOCK
    n_blk = seq // blk

    key_i = lax.broadcasted_iota(jnp.int32, (blk, blk), 0)
    qry_i = lax.broadcasted_iota(jnp.int32, (blk, blk), 1)
    causal = key_i <= qry_i
    blk_id = lax.broadcasted_iota(jnp.int32, (n_blk, blk), 0)

    def head_cols(hd):
        return slice(hd * HEAD_DIM, (hd + 1) * HEAD_DIM)

    blocks = [(hd, i, j) for hd in range(heads)
              for i in range(n_blk - 1, -1, -1) for j in ([i] + list(range(i)))]
    last_of_tile = {i: (i - 1 if i else 0) for i in range(n_blk)}
    km_rows, raw, gate_raw, bias, scores, col_max, p_bf, acc, row_sum = {}, {}, {}, {}, {}, {}, {}, {}, {}

    def mean_keys(hd):
        kf = k_ref[:, head_cols(hd)].astype(F32)
        k_mean = jnp.concatenate(
            [jnp.mean(kf[n * blk:(n + 1) * blk, :], axis=0, keepdims=True) for n in range(n_blk)], axis=0)
        km_hi = k_mean.astype(BF16)
        km_lo = (k_mean - km_hi.astype(F32)).astype(BF16)
        km_rows[hd] = jnp.concatenate([km_hi, km_lo], axis=0)

    def transpose_values(hd):
        vt_ref[hd] = v_ref[:, head_cols(hd)].astype(F32).T.astype(BF16)

    def score_matmul(nb):
        hd, i, j = blocks[nb]
        q_i = q_ref[i * blk:(i + 1) * blk, head_cols(hd)]
        k_j = k_ref[j * blk:(j + 1) * blk, head_cols(hd)]
        if j == i:
            r = lax.dot_general(jnp.concatenate([k_j, km_rows[hd]], axis=0), q_i, _NT,
                                preferred_element_type=F32)
            raw[nb] = r[0:blk, :]
            gate_raw[hd, i] = r[blk:blk + n_blk, :] + r[blk + n_blk:blk + 2 * n_blk, :]
        else:
            raw[nb] = lax.dot_general(k_j, q_i, _NT, preferred_element_type=F32)

    def select_blocks(hd, i):
        past = blk_id < i
        g = jnp.where(past, gate_raw.pop((hd, i)), NEG)
        rank = jnp.zeros((n_blk, blk), jnp.int32)
        for other in range(n_blk):
            g_o = g[other:other + 1, :]
            beats = (g_o > g) | ((g_o == g) & (other < blk_id))
            rank = rank + beats.astype(jnp.int32)
        bias[hd, i] = jnp.where(past & (rank < MOBA_TOPK), 0.0, NEG).astype(F32)

    def mask_and_max(nb):
        hd, i, j = blocks[nb]
        s = raw.pop(nb)
        if j == i:
            s = jnp.where(causal, s, NEG)
            select_blocks(hd, i)
        else:
            s = s + bias[hd, i][j:j + 1, :]
        scores[nb] = s
        cm = jnp.max(s, axis=0, keepdims=True)
        col_max[hd, i] = cm if j == i else jnp.maximum(col_max[hd, i], cm)

    def exponentiate(nb):
        hd, i, j = blocks[nb]
        p = jnp.exp2(scores.pop(nb) - col_max[hd, i])
        ps = jnp.sum(p, axis=0, keepdims=True)
        row_sum[hd, i] = ps if j == i else row_sum[hd, i] + ps
        p_bf[nb] = p.astype(BF16)

    def value_matmul(nb):
        hd, i, j = blocks[nb]
        d = jnp.dot(vt_ref[hd, :, j * blk:(j + 1) * blk], p_bf.pop(nb), preferred_element_type=F32)
        acc[hd, i] = d if j == i else acc[hd, i] + d
        if j == last_of_tile[i]:
            out = (acc.pop((hd, i)) / row_sum.pop((hd, i))).T
            o_ref[i * blk:(i + 1) * blk, head_cols(hd)] = out.astype(o_ref.dtype)

    def cast_side():
        side_bf_ref[...] = side_ref[...].astype(BF16)
        side2_bf_ref[...] = side2_ref[...].astype(BF16)

    mask_step = [nb + _MOBA_LAG_MASK for nb in range(len(blocks))]
    tile_done = {}
    for nb, (hd, i, _) in enumerate(blocks):
        tile_done[hd, i] = max(tile_done.get((hd, i), 0), mask_step[nb])
    per_head = len(blocks) // heads
    events = [(2, 1, cast_side)]
    for hd in range(heads):
        first = hd * per_head
        events += [(max(first - _MOBA_LAG_MASK, 0), -1, functools.partial(mean_keys, hd)),
                   (first + 1, 1, functools.partial(transpose_values, hd))]
    exp_step = -1
    for nb, (hd, i, _) in enumerate(blocks):
        exp_step = max(exp_step + 1, tile_done[hd, i] + _MOBA_LAG_EXP)
        events += [
            (nb, 0, functools.partial(score_matmul, nb)),
            (exp_step + _MOBA_LAG_PV, 2, functools.partial(value_matmul, nb)),
            (exp_step, 3, functools.partial(exponentiate, nb)),
            (mask_step[nb], 4, functools.partial(mask_and_max, nb)),
        ]
    _emit_in_order(events)


def _moba_attn(qkv, side, side2, *, batch, seq, heads_per_step):
    kern = functools.partial(_moba_kernel, seq=seq, heads=heads_per_step)
    groups = N_HEADS_MOBA // heads_per_step
    hq, hk, hv = 0, groups, 2 * groups
    blk = (seq, heads_per_step * HEAD_DIM)
    slab_of_step = lambda b, h: (b * groups + h, 0)
    side_in, side_out, side_shape = _side_cast_specs(side, batch * groups, slab_of_step)
    side2_in, side2_out, side2_shape = _side_cast_specs(side2, batch * groups, slab_of_step)
    return pl.pallas_call(
        kern,
        out_shape=(jax.ShapeDtypeStruct((batch * seq, N_HEADS_MOBA * HEAD_DIM), ATTN_OUT_DTYPE),
                   side_shape, side2_shape),
        grid=(batch, groups),
        in_specs=[
            pl.BlockSpec(blk, lambda b, h: (b, hq + h)),
            pl.BlockSpec(blk, lambda b, h: (b, hk + h)),
            pl.BlockSpec(blk, lambda b, h: (b, hv + h)),
            side_in, side2_in,
        ],
        out_specs=(pl.BlockSpec(blk, lambda b, h: (b, h)), side_out, side2_out),
        scratch_shapes=[pltpu.VMEM((heads_per_step, HEAD_DIM, seq), BF16)],
        compiler_params=pltpu.CompilerParams(
            dimension_semantics=("arbitrary", "arbitrary"),
            vmem_limit_bytes=VMEM_LIMIT_BYTES),
        name="moba_attn",
    )(qkv, qkv, qkv, side, side2)


_SB_LAGS = (1, 1, 2, 1)


def _sb_kernel(q_ref, k_ref, v_ref, side_ref, o_ref, side_bf_ref, vt_ref, *, seq, tile, heads):
    n_tiles = seq // tile
    key_i = lax.broadcasted_iota(jnp.int32, (tile, tile), 0)
    qry_i = lax.broadcasted_iota(jnp.int32, (tile, tile), 1)
    causal = key_i < qry_i
    this_or_later = (qry_i >= key_i).astype(BF16)

    def head_cols(hd):
        return slice(hd * HEAD_DIM, (hd + 1) * HEAD_DIM)

    blocks = [(hd, i, j) for hd in range(heads)
              for i in range(n_tiles - 1, -1, -1) for j in range(i, -1, -1)]
    raw, logit, soft_bf, later_sum, a_bf, acc, carry = {}, {}, {}, {}, {}, {}, {}

    def logit_matmul(nb):
        hd, i, j = blocks[nb]
        raw[nb] = lax.dot_general(k_ref[j * tile:(j + 1) * tile, head_cols(hd)],
                                  q_ref[i * tile:(i + 1) * tile, head_cols(hd)],
                                  _NT, preferred_element_type=F32)

    def softplus(nb):
        _, i, j = blocks[nb]
        z = raw.pop(nb)
        t = jnp.maximum(z, 0.0) + jnp.log2(1.0 + jnp.exp2(-jnp.abs(z)))
        if j == i:
            t = jnp.where(causal, t, 0.0)
        logit[nb] = z
        soft_bf[nb] = t.astype(BF16)

    def cumsum_matmul(nb):
        later_sum[nb] = jnp.dot(this_or_later, soft_bf.pop(nb), preferred_element_type=F32)

    def weights(nb):
        hd, i, j = blocks[nb]
        inc = later_sum.pop(nb)
        x = logit.pop(nb) - inc
        if j != i:
            x = x - carry[hd, i]
        a = jnp.exp2(x)
        if j == i:
            a = jnp.where(causal, a, 0.0)
        total = inc[0:1, :]
        carry[hd, i] = total if j == i else carry[hd, i] + total
        a_bf[nb] = a.astype(BF16)

    def value_matmul(nb):
        hd, i, j = blocks[nb]
        d = jnp.dot(vt_ref[hd, :, j * tile:(j + 1) * tile], a_bf.pop(nb), preferred_element_type=F32)
        acc[hd, i] = d if j == i else acc[hd, i] + d
        if j == 0:
            o_ref[i * tile:(i + 1) * tile, head_cols(hd)] = acc.pop((hd, i)).T.astype(o_ref.dtype)

    def transpose_values(hd):
        vt_ref[hd] = v_ref[:, head_cols(hd)].astype(F32).T.astype(BF16)

    def cast_side():
        side_bf_ref[...] = side_ref[...].astype(BF16)

    l_soft, l_cum, l_w, l_pv = _SB_LAGS
    per_head = len(blocks) // heads
    events = [(2, 1, cast_side)]
    events += [(hd * per_head + 1, 1, functools.partial(transpose_values, hd)) for hd in range(heads)]
    for nb in range(len(blocks)):
        events += [
            (nb, 0, functools.partial(logit_matmul, nb)),
            (nb + l_soft + l_cum + l_w + l_pv, 2, functools.partial(value_matmul, nb)),
            (nb + l_soft + l_cum, 3, functools.partial(cumsum_matmul, nb)),
            (nb + l_soft + l_cum + l_w, 4, functools.partial(weights, nb)),
            (nb + l_soft, 5, functools.partial(softplus, nb)),
        ]
    _emit_in_order(events)


def _sb_attn(qkv, side, *, batch, seq, tile, heads_per_step):
    kern = functools.partial(_sb_kernel, seq=seq, tile=tile, heads=heads_per_step)
    groups = N_HEADS_SB // heads_per_step
    base = 3 * (N_HEADS_MOBA // heads_per_step)
    hq, hk, hv = base, base + groups, base + 2 * groups
    blk = (seq, heads_per_step * HEAD_DIM)
    side_in, side_out, side_shape = _side_cast_specs(
        side, batch * groups, lambda b, h: (b * groups + h, 0))
    return pl.pallas_call(
        kern,
        out_shape=(jax.ShapeDtypeStruct((batch * seq, N_HEADS_SB * HEAD_DIM), ATTN_OUT_DTYPE), side_shape),
        grid=(batch, groups),
        in_specs=[
            pl.BlockSpec(blk, lambda b, h: (b, hq + h)),
            pl.BlockSpec(blk, lambda b, h: (b, hk + h)),
            pl.BlockSpec(blk, lambda b, h: (b, hv + h)),
            side_in,
        ],
        out_specs=(pl.BlockSpec(blk, lambda b, h: (b, h)), side_out),
        scratch_shapes=[pltpu.VMEM((heads_per_step, HEAD_DIM, seq), BF16)],
        compiler_params=pltpu.CompilerParams(
            dimension_semantics=("arbitrary", "arbitrary"),
            vmem_limit_bytes=VMEM_LIMIT_BYTES),
        name="sb_attn",
    )(qkv, qkv, qkv, side)


def _out_proj_kernel(oa_ref, ob_ref, ga_ref, gb_ref, w_ref, x_ref, y_ref):
    d_a = oa_ref.shape[1]
    oa = oa_ref[...].astype(F32)
    ob = ob_ref[...].astype(F32)
    na = ((oa * _rms_scale(oa)) * ga_ref[...]).astype(BF16)
    nb = ((ob * _rms_scale(ob)) * gb_ref[...]).astype(BF16)
    y = (jnp.dot(na, w_ref[0:d_a, :], preferred_element_type=F32)
         + jnp.dot(nb, w_ref[d_a:, :], preferred_element_type=F32))
    y_ref[...] = x_ref[...] + y


def _out_proj(o_a, o_b, g_a, g_b, w_bf16, x2, *, tm):
    m, d_a = o_a.shape
    d_b = o_b.shape[1]
    d = w_bf16.shape[1]
    return pl.pallas_call(
        _out_proj_kernel,
        out_shape=jax.ShapeDtypeStruct((m, d), F32),
        grid=(m // tm,),
        in_specs=[
            pl.BlockSpec((tm, d_a), lambda i: (i, 0)),
            pl.BlockSpec((tm, d_b), lambda i: (i, 0)),
            pl.BlockSpec((1, d_a), lambda i: (0, 0)),
            pl.BlockSpec((1, d_b), lambda i: (0, 0)),
            pl.BlockSpec((d_a + d_b, d), lambda i: (0, 0)),
            pl.BlockSpec((tm, d), lambda i: (i, 0)),
        ],
        out_specs=pl.BlockSpec((tm, d), lambda i: (i, 0)),
        compiler_params=pltpu.CompilerParams(
            dimension_semantics=("arbitrary",),
            vmem_limit_bytes=VMEM_LIMIT_BYTES),
        name="out_proj",
    )(o_a, o_b, g_a, g_b, w_bf16, x2)


def _mlp_kernel(x_ref, g_ref, wu_ref, wd_ref, gf_ref, o_ref, h_ref, *, n_chunk):
    f = pl.program_id(1)
    n_f = pl.num_programs(1)

    @pl.when(f == 0)
    def _():
        x = x_ref[...]
        h_ref[...] = ((x * _rms_scale(x)) * g_ref[...]).astype(BF16)
        o_ref[...] = x

    u = jnp.dot(h_ref[...], wu_ref[...], preferred_element_type=F32)
    r = jnp.maximum(u, 0.0)
    act = (r * r).astype(BF16)
    d = o_ref.shape[1]
    cw = d // n_chunk
    for c in range(n_chunk):
        o_ref[:, c * cw:(c + 1) * cw] += jnp.dot(
            act, wd_ref[:, c * cw:(c + 1) * cw], preferred_element_type=F32)

    @pl.when(f == n_f - 1)
    def _():
        y = o_ref[...]
        o_ref[...] = (y * _rms_scale(y)) * gf_ref[...]


def _mlp(x1, g, wu_bf16, wd_bf16, g_final, *, tm, tf):
    m, d = x1.shape
    d_ff = wu_bf16.shape[1]
    kern = functools.partial(_mlp_kernel, n_chunk=4)
    return pl.pallas_call(
        kern,
        out_shape=jax.ShapeDtypeStruct((m, d), F32),
        grid=(m // tm, d_ff // tf),
        in_specs=[
            pl.BlockSpec((tm, d), lambda i, f: (i, 0)),
            pl.BlockSpec((1, d), lambda i, f: (0, 0)),
            pl.BlockSpec((d, tf), lambda i, f: (0, f)),
            pl.BlockSpec((tf, d), lambda i, f: (f, 0)),
            pl.BlockSpec((1, d), lambda i, f: (0, 0)),
        ],
        out_specs=pl.BlockSpec((tm, d), lambda i, f: (i, 0)),
        scratch_shapes=[pltpu.VMEM((tm, d), BF16)],
        compiler_params=pltpu.CompilerParams(
            dimension_semantics=("arbitrary", "arbitrary"),
            vmem_limit_bytes=LARGE_VMEM_LIMIT_BYTES),
        name="mlp",
    )(x1, g, wu_bf16, wd_bf16, g_final)


def _rope_tables(seq):
    half = ROPE_DIMS // 2
    inv_freq = ROPE_THETA ** (-jnp.arange(half, dtype=F32) / half)
    ang = jnp.arange(seq, dtype=F32)[:, None] * inv_freq[None, :]
    cos, sin = jnp.cos(ang), jnp.sin(ang)
    ones = jnp.ones((seq, HEAD_DIM - ROPE_DIMS), F32)
    zeros_rest = jnp.zeros((seq, HEAD_DIM - ROPE_DIMS), F32)
    cos_t = jnp.concatenate([cos, cos, ones], axis=1)
    sin_t = jnp.concatenate([-sin, sin, zeros_rest], axis=1)
    return cos_t, sin_t


def kernel(x, mix_norm_g, w_in, moba_out_g, sb_out_g, w_out, mlp_norm_g, w_up, w_down, final_norm_g):
    batch, seq, d_model = x.shape
    depth = w_in.shape[0]
    cos_t, sin_t = _rope_tables(seq)
    x2 = x.reshape(batch * seq, d_model)
    for l in range(depth):
        qkv = _qkv_proj(x2, mix_norm_g[l][None, :], w_in[l], cos_t, sin_t,
                        seq=seq, tm=1024, tn=1024, n_groups=2)
        o_a, w_up_bf, w_out_bf = _moba_attn(qkv, w_up[l], w_out[l], batch=batch, seq=seq,
                                            heads_per_step=ATTN_HEADS_PER_STEP)
        o_b, w_down_bf = _sb_attn(qkv, w_down[l], batch=batch, seq=seq, tile=256,
                                  heads_per_step=ATTN_HEADS_PER_STEP)
        x1 = _out_proj(o_a, o_b, moba_out_g[l][None, :], sb_out_g[l][None, :],
                       w_out_bf, x2, tm=512)
        last = l == depth - 1
        assert last, "kernel fuses the final RMSNorm into the last layer's MLP; DEPTH must be 1"
        x2 = _mlp(x1, mlp_norm_g[l][None, :], w_up_bf, w_down_bf,
                  final_norm_g[None, :], tm=1024, tf=1024)
    return x2.reshape(batch, seq, d_model)
```

```python
import functools
import math

import jax
import jax.numpy as jnp
import numpy as np
from jax import lax
from jax.experimental import pallas as pl
from jax.experimental.pallas import tpu as pltpu

HEAD_DIM = 128
N_HEADS_MOBA = 8
N_HEADS_SB = 8
MOBA_BLOCK = 256
MOBA_TOPK = 3
ROPE_THETA = 500000.0
ROPE_DIMS = HEAD_DIM // 4
EPS = 1e-6
NEG = -1e30

F32 = jnp.float32
BF16 = jnp.bfloat16
ATTN_OUT_DTYPE = BF16
ATTN_HEADS_PER_STEP = 2

_NT = (((1,), (1,)), ((), ()))

VMEM_LIMIT_BYTES = 56 * 1024 * 1024
LARGE_VMEM_LIMIT_BYTES = 62 * 1024 * 1024

Q_SCALE = HEAD_DIM ** -0.5 * math.log2(math.e)


def _rms_scale(x):
    return lax.rsqrt(jnp.mean(x * x, axis=-1, keepdims=True) + EPS)


def _emit_in_order(events):
    for _, _, thunk in sorted(events, key=lambda e: (e[0], e[1])):
        thunk()


def _qkv_kernel(x_ref, g_ref, w_ref, cos_ref, sin_ref, o_ref, h_ref):
    j = pl.program_id(1)
    i = pl.program_id(2)
    tm = x_ref.shape[0]
    rows = pl.ds(pl.multiple_of(i * tm, tm), tm)

    def project(h=None):
        h = h_ref[rows, :] if h is None else h
        return jnp.dot(h, w_ref[...].astype(BF16), preferred_element_type=F32)

    def rope(y, post_scale):
        n_heads = y.shape[1] // HEAD_DIM
        half = ROPE_DIMS // 2
        lane = lax.broadcasted_iota(jnp.int32, (y.shape[0], HEAD_DIM), 1)
        partner = jnp.where(lane < ROPE_DIMS, lane ^ half, lane)
        for hd in range(n_heads):
            t = y[:, hd * HEAD_DIM:(hd + 1) * HEAD_DIM]
            swapped = jnp.take_along_axis(t, partner, axis=1)
            r = t * cos_ref[...] + swapped * sin_ref[...]
            if post_scale is not None:
                r = r * post_scale
            o_ref[:, hd * HEAD_DIM:(hd + 1) * HEAD_DIM] = r.astype(o_ref.dtype)

    @pl.when(j == 0)
    def _():
        x = x_ref[...]
        h = ((x * _rms_scale(x)) * g_ref[...]).astype(BF16)
        h_ref[rows, :] = h
        rope(project(h), Q_SCALE)

    @pl.when(j == 1)
    def _():
        rope(project(), None)

    @pl.when(j == 3)
    def _():
        o_ref[...] = (project() * Q_SCALE).astype(o_ref.dtype)

    @pl.when((j == 2) | (j >= 4))
    def _():
        o_ref[...] = project().astype(o_ref.dtype)


def _side_cast_specs(side, n_steps, index_map):
    rows, cols = side.shape
    slab = rows // n_steps
    assert slab * n_steps == rows and slab % 16 == 0, (rows, n_steps)
    spec = pl.BlockSpec((slab, cols), index_map)
    return spec, spec, jax.ShapeDtypeStruct((rows, cols), BF16)


def _qkv_proj(x2, g, w, cos_t, sin_t, *, seq, tm, tn, n_groups):
    m, d = x2.shape
    n = w.shape[1]
    tiles = m // (tm * n_groups)
    pos_blocks = seq // tm
    n_rope_tiles = 2

    def x_tile(gr, j, i):
        return gr * tiles + jnp.where(j == 0, i, tiles - 1)

    tab_spec = pl.BlockSpec(
        (tm, HEAD_DIM), lambda gr, j, i: (jnp.where(j < n_rope_tiles, (gr * tiles + i) % pos_blocks, 0), 0))
    return pl.pallas_call(
        _qkv_kernel,
        out_shape=jax.ShapeDtypeStruct((m, n), BF16),
        grid=(n_groups, n // tn, tiles),
        in_specs=[
            pl.BlockSpec((tm, d), lambda gr, j, i: (x_tile(gr, j, i), 0)),
            pl.BlockSpec((1, d), lambda gr, j, i: (0, 0)),
            pl.BlockSpec((d, tn), lambda gr, j, i: (0, j)),
            tab_spec, tab_spec,
        ],
        out_specs=pl.BlockSpec((tm, tn), lambda gr, j, i: (gr * tiles + i, j)),
        scratch_shapes=[pltpu.VMEM((tiles * tm, d), BF16)],
        compiler_params=pltpu.CompilerParams(
            dimension_semantics=("arbitrary", "arbitrary", "arbitrary"),
            vmem_limit_bytes=LARGE_VMEM_LIMIT_BYTES),
        name="qkv_proj",
    )(x2, g, w, cos_t, sin_t)


_MOBA_LAG_MASK, _MOBA_LAG_EXP, _MOBA_LAG_PV = 3, 2, 2


def _moba_kernel(q_ref, k_ref, v_ref, side_ref, side2_ref, o_ref, side_bf_ref, side2_bf_ref, vt_ref,
                 *, seq, heads):
    blk = MOBA_BLOCK
    n_blk = seq // blk

    key_i = lax.broadcasted_iota(jnp.int32, (blk, blk), 0)
    qry_i = lax.broadcasted_iota(jnp.int32, (blk, blk), 1)
    causal = key_i <= qry_i
    blk_id = lax.broadcasted_iota(jnp.int32, (n_blk, blk), 0)

    def head_cols(hd):
        return slice(hd * HEAD_DIM, (hd + 1) * HEAD_DIM)

    blocks = [(hd, i, j) for hd in range(heads)
              for i in range(n_blk - 1, -1, -1) for j in ([i] + list(range(i)))]
    last_of_tile = {i: (i - 1 if i else 0) for i in range(n_blk)}
    km_rows, raw, gate_raw, bias, scores, col_max, p_bf, acc, row_sum = {}, {}, {}, {}, {}, {}, {}, {}, {}

    def mean_keys(hd):
        kf = k_ref[:, head_cols(hd)].astype(F32)
        k_mean = jnp.concatenate(
            [jnp.mean(kf[n * blk:(n + 1) * blk, :], axis=0, keepdims=True) for n in range(n_blk)], axis=0)
        km_hi = k_mean.astype(BF16)
        km_lo = (k_mean - km_hi.astype(F32)).astype(BF16)
        km_rows[hd] = jnp.concatenate([km_hi, km_lo], axis=0)

    def transpose_values(hd):
        vt_ref[hd] = v_ref[:, head_cols(hd)].astype(F32).T.astype(BF16)

    def score_matmul(nb):
        hd, i, j = blocks[nb]
        q_i = q_ref[i * blk:(i + 1) * blk, head_cols(hd)]
        k_j = k_ref[j * blk:(j + 1) * blk, head_cols(hd)]
        if j == i:
            r = lax.dot_general(jnp.concatenate([k_j, km_rows[hd]], axis=0), q_i, _NT,
                                preferred_element_type=F32)
            raw[nb] = r[0:blk, :]
            gate_raw[hd, i] = r[blk:blk + n_blk, :] + r[blk + n_blk:blk + 2 * n_blk, :]
        else:
            raw[nb] = lax.dot_general(k_j, q_i, _NT, preferred_element_type=F32)

    def select_blocks(hd, i):
        past = blk_id < i
        g = jnp.where(past, gate_raw.pop((hd, i)), NEG)
        rank = jnp.zeros((n_blk, blk), jnp.int32)
        for other in range(n_blk):
            g_o = g[other:other + 1, :]
            beats = (g_o > g) | ((g_o == g) & (other < blk_id))
            rank = rank + beats.astype(jnp.int32)
        bias[hd, i] = jnp.where(past & (rank < MOBA_TOPK), 0.0, NEG).astype(F32)

    def mask_and_max(nb):
        hd, i, j = blocks[nb]
        s = raw.pop(nb)
        if j == i:
            s = jnp.where(causal, s, NEG)
            select_blocks(hd, i)
        else:
            s = s + bias[hd, i][j:j + 1, :]
        scores[nb] = s
        cm = jnp.max(s, axis=0, keepdims=True)
        col_max[hd, i] = cm if j == i else jnp.maximum(col_max[hd, i], cm)

    def exponentiate(nb):
        hd, i, j = blocks[nb]
        p = jnp.exp2(scores.pop(nb) - col_max[hd, i])
        ps = jnp.sum(p, axis=0, keepdims=True)
        row_sum[hd, i] = ps if j == i else row_sum[hd, i] + ps
        p_bf[nb] = p.astype(BF16)

    def value_matmul(nb):
        hd, i, j = blocks[nb]
        d = jnp.dot(vt_ref[hd, :, j * blk:(j + 1) * blk], p_bf.pop(nb), preferred_element_type=F32)
        acc[hd, i] = d if j == i else acc[hd, i] + d
        if j == last_of_tile[i]:
            out = (acc.pop((hd, i)) / row_sum.pop((hd, i))).T
            o_ref[i * blk:(i + 1) * blk, head_cols(hd)] = out.astype(o_ref.dtype)

    def cast_side():
        side_bf_ref[...] = side_ref[...].astype(BF16)
        side2_bf_ref[...] = side2_ref[...].astype(BF16)

    mask_step = [nb + _MOBA_LAG_MASK for nb in range(len(blocks))]
    tile_done = {}
    for nb, (hd, i, _) in enumerate(blocks):
        tile_done[hd, i] = max(tile_done.get((hd, i), 0), mask_step[nb])
    per_head = len(blocks) // heads
    events = [(2, 1, cast_side)]
    for hd in range(heads):
        first = hd * per_head
        events += [(max(first - _MOBA_LAG_MASK, 0), -1, functools.partial(mean_keys, hd)),
                   (first + 1, 1, functools.partial(transpose_values, hd))]
    exp_step = -1
    for nb, (hd, i, _) in enumerate(blocks):
        exp_step = max(exp_step + 1, tile_done[hd, i] + _MOBA_LAG_EXP)
        events += [
            (nb, 0, functools.partial(score_matmul, nb)),
            (exp_step + _MOBA_LAG_PV, 2, functools.partial(value_matmul, nb)),
            (exp_step, 3, functools.partial(exponentiate, nb)),
            (mask_step[nb], 4, functools.partial(mask_and_max, nb)),
        ]
    _emit_in_order(events)


def _moba_attn(qkv, side, side2, *, batch, seq, heads_per_step):
    kern = functools.partial(_moba_kernel, seq=seq, heads=heads_per_step)
    groups = N_HEADS_MOBA // heads_per_step
    hq, hk, hv = 0, groups, 2 * groups
    blk = (seq, heads_per_step * HEAD_DIM)
    slab_of_step = lambda b, h: (b * groups + h, 0)
    side_in, side_out, side_shape = _side_cast_specs(side, batch * groups, slab_of_step)
    side2_in, side2_out, side2_shape = _side_cast_specs(side2, batch * groups, slab_of_step)
    return pl.pallas_call(
        kern,
        out_shape=(jax.ShapeDtypeStruct((batch * seq, N_HEADS_MOBA * HEAD_DIM), ATTN_OUT_DTYPE),
                   side_shape, side2_shape),
        grid=(batch, groups),
        in_specs=[
            pl.BlockSpec(blk, lambda b, h: (b, hq + h)),
            pl.BlockSpec(blk, lambda b, h: (b, hk + h)),
            pl.BlockSpec(blk, lambda b, h: (b, hv + h)),
            side_in, side2_in,
        ],
        out_specs=(pl.BlockSpec(blk, lambda b, h: (b, h)), side_out, side2_out),
        scratch_shapes=[pltpu.VMEM((heads_per_step, HEAD_DIM, seq), BF16)],
        compiler_params=pltpu.CompilerParams(
            dimension_semantics=("arbitrary", "arbitrary"),
            vmem_limit_bytes=VMEM_LIMIT_BYTES),
        name="moba_attn",
    )(qkv, qkv, qkv, side, side2)


_SB_LAGS = (1, 1, 2, 1)


def _sb_kernel(q_ref, k_ref, v_ref, side_ref, o_ref, side_bf_ref, vt_ref, *, seq, tile, heads):
    n_tiles = seq // tile
    key_i = lax.broadcasted_iota(jnp.int32, (tile, tile), 0)
    qry_i = lax.broadcasted_iota(jnp.int32, (tile, tile), 1)
    causal = key_i < qry_i
    this_or_later = (qry_i >= key_i).astype(BF16)

    def head_cols(hd):
        return slice(hd * HEAD_DIM, (hd + 1) * HEAD_DIM)

    blocks = [(hd, i, j) for hd in range(heads)
              for i in range(n_tiles - 1, -1, -1) for j in range(i, -1, -1)]
    raw, logit, soft_bf, later_sum, a_bf, acc, carry = {}, {}, {}, {}, {}, {}, {}

    def logit_matmul(nb):
        hd, i, j = blocks[nb]
        raw[nb] = lax.dot_general(k_ref[j * tile:(j + 1) * tile, head_cols(hd)],
                                  q_ref[i * tile:(i + 1) * tile, head_cols(hd)],
                                  _NT, preferred_element_type=F32)

    def softplus(nb):
        _, i, j = blocks[nb]
        z = raw.pop(nb)
        t = jnp.maximum(z, 0.0) + jnp.log2(1.0 + jnp.exp2(-jnp.abs(z)))
        if j == i:
            t = jnp.where(causal, t, 0.0)
        logit[nb] = z
        soft_bf[nb] = t.astype(BF16)

    def cumsum_matmul(nb):
        later_sum[nb] = jnp.dot(this_or_later, soft_bf.pop(nb), preferred_element_type=F32)

    def weights(nb):
        hd, i, j = blocks[nb]
        inc = later_sum.pop(nb)
        x = logit.pop(nb) - inc
        if j != i:
            x = x - carry[hd, i]
        a = jnp.exp2(x)
        if j == i:
            a = jnp.where(causal, a, 0.0)
        total = inc[0:1, :]
        carry[hd, i] = total if j == i else carry[hd, i] + total
        a_bf[nb] = a.astype(BF16)

    def value_matmul(nb):
        hd, i, j = blocks[nb]
        d = jnp.dot(vt_ref[hd, :, j * tile:(j + 1) * tile], a_bf.pop(nb), preferred_element_type=F32)
        acc[hd, i] = d if j == i else acc[hd, i] + d
        if j == 0:
            o_ref[i * tile:(i + 1) * tile, head_cols(hd)] = acc.pop((hd, i)).T.astype(o_ref.dtype)

    def transpose_values(hd):
        vt_ref[hd] = v_ref[:, head_cols(hd)].astype(F32).T.astype(BF16)

    def cast_side():
        side_bf_ref[...] = side_ref[...].astype(BF16)

    l_soft, l_cum, l_w, l_pv = _SB_LAGS
    per_head = len(blocks) // heads
    events = [(2, 1, cast_side)]
    events += [(hd * per_head + 1, 1, functools.partial(transpose_values, hd)) for hd in range(heads)]
    for nb in range(len(blocks)):
        events += [
            (nb, 0, functools.partial(logit_matmul, nb)),
            (nb + l_soft + l_cum + l_w + l_pv, 2, functools.partial(value_matmul, nb)),
            (nb + l_soft + l_cum, 3, functools.partial(cumsum_matmul, nb)),
            (nb + l_soft + l_cum + l_w, 4, functools.partial(weights, nb)),
            (nb + l_soft, 5, functools.partial(softplus, nb)),
        ]
    _emit_in_order(events)


def _sb_attn(qkv, side, *, batch, seq, tile, heads_per_step):
    kern = functools.partial(_sb_kernel, seq=seq, tile=tile, heads=heads_per_step)
    groups = N_HEADS_SB // heads_per_step
    base = 3 * (N_HEADS_MOBA // heads_per_step)
    hq, hk, hv = base, base + groups, base + 2 * groups
    blk = (seq, heads_per_step * HEAD_DIM)
    side_in, side_out, side_shape = _side_cast_specs(
        side, batch * groups, lambda b, h: (b * groups + h, 0))
    return pl.pallas_call(
        kern,
        out_shape=(jax.ShapeDtypeStruct((batch * seq, N_HEADS_SB * HEAD_DIM), ATTN_OUT_DTYPE), side_shape),
        grid=(batch, groups),
        in_specs=[
            pl.BlockSpec(blk, lambda b, h: (b, hq + h)),
            pl.BlockSpec(blk, lambda b, h: (b, hk + h)),
            pl.BlockSpec(blk, lambda b, h: (b, hv + h)),
            side_in,
        ],
        out_specs=(pl.BlockSpec(blk, lambda b, h: (b, h)), side_out),
        scratch_shapes=[pltpu.VMEM((heads_per_step, HEAD_DIM, seq), BF16)],
        compiler_params=pltpu.CompilerParams(
            dimension_semantics=("arbitrary", "arbitrary"),
            vmem_limit_bytes=VMEM_LIMIT_BYTES),
        name="sb_attn",
    )(qkv, qkv, qkv, side)


def _out_proj_kernel(oa_ref, ob_ref, ga_ref, gb_ref, w_ref, x_ref, y_ref):
    d_a = oa_ref.shape[1]
    oa = oa_ref[...].astype(F32)
    ob = ob_ref[...].astype(F32)
    na = ((oa * _rms_scale(oa)) * ga_ref[...]).astype(BF16)
    nb = ((ob * _rms_scale(ob)) * gb_ref[...]).astype(BF16)
    y = (jnp.dot(na, w_ref[0:d_a, :], preferred_element_type=F32)
         + jnp.dot(nb, w_ref[d_a:, :], preferred_element_type=F32))
    y_ref[...] = x_ref[...] + y


def _out_proj(o_a, o_b, g_a, g_b, w_bf16, x2, *, tm):
    m, d_a = o_a.shape
    d_b = o_b.shape[1]
    d = w_bf16.shape[1]
    return pl.pallas_call(
        _out_proj_kernel,
        out_shape=jax.ShapeDtypeStruct((m, d), F32),
        grid=(m // tm,),
        in_specs=[
            pl.BlockSpec((tm, d_a), lambda i: (i, 0)),
            pl.BlockSpec((tm, d_b), lambda i: (i, 0)),
            pl.BlockSpec((1, d_a), lambda i: (0, 0)),
            pl.BlockSpec((1, d_b), lambda i: (0, 0)),
            pl.BlockSpec((d_a + d_b, d), lambda i: (0, 0)),
            pl.BlockSpec((tm, d), lambda i: (i, 0)),
        ],
        out_specs=pl.BlockSpec((tm, d), lambda i: (i, 0)),
        compiler_params=pltpu.CompilerParams(
            dimension_semantics=("arbitrary",),
            vmem_limit_bytes=VMEM_LIMIT_BYTES),
        name="out_proj",
    )(o_a, o_b, g_a, g_b, w_bf16, x2)


def _mlp_kernel(x_ref, g_ref, wu_ref, wd_ref, gf_ref, o_ref, h_ref, *, n_chunk):
    f = pl.program_id(1)
    n_f = pl.num_programs(1)

    cw = o_ref.shape[1] // n_chunk

    def mlp_slice(h, base):
        u = jnp.dot(h, wu_ref[...], preferred_element_type=F32)
        r = jnp.maximum(u, 0.0)
        act = (r * r).astype(BF16)
        for c in range(n_chunk):
            cols = slice(c * cw, (c + 1) * cw)
            o_ref[:, cols] = base(cols) + jnp.dot(act, wd_ref[:, cols], preferred_element_type=F32)

    @pl.when(f == 0)
    def _():
        x = x_ref[...]
        h = ((x * _rms_scale(x)) * g_ref[...]).astype(BF16)
        h_ref[...] = h
        mlp_slice(h, lambda cols: x_ref[:, cols])

    @pl.when(f > 0)
    def _():
        mlp_slice(h_ref[...], lambda cols: o_ref[:, cols])

    @pl.when(f == n_f - 1)
    def _():
        y = o_ref[...]
        o_ref[...] = (y * _rms_scale(y)) * gf_ref[...]


def _mlp(x1, g, wu_bf16, wd_bf16, g_final, *, tm, tf):
    m, d = x1.shape
    d_ff = wu_bf16.shape[1]
    kern = functools.partial(_mlp_kernel, n_chunk=4)
    return pl.pallas_call(
        kern,
        out_shape=jax.ShapeDtypeStruct((m, d), F32),
        grid=(m // tm, d_ff // tf),
        in_specs=[
            pl.BlockSpec((tm, d), lambda i, f: (i, 0)),
            pl.BlockSpec((1, d), lambda i, f: (0, 0)),
            pl.BlockSpec((d, tf), lambda i, f: (0, f)),
            pl.BlockSpec((tf, d), lambda i, f: (f, 0)),
            pl.BlockSpec((1, d), lambda i, f: (0, 0)),
        ],
        out_specs=pl.BlockSpec((tm, d), lambda i, f: (i, 0)),
        scratch_shapes=[pltpu.VMEM((tm, d), BF16)],
        compiler_params=pltpu.CompilerParams(
            dimension_semantics=("arbitrary", "arbitrary"),
            vmem_limit_bytes=LARGE_VMEM_LIMIT_BYTES),
        name="mlp",
    )(x1, g, wu_bf16, wd_bf16, g_final)


def _rope_tables(seq):
    half = ROPE_DIMS // 2
    inv_freq = ROPE_THETA ** (-np.arange(half, dtype=np.float64) / half)
    ang = np.arange(seq, dtype=np.float64)[:, None] * inv_freq[None, :]
    cos, sin = np.cos(ang), np.sin(ang)
    ones = np.ones((seq, HEAD_DIM - ROPE_DIMS))
    zeros_rest = np.zeros((seq, HEAD_DIM - ROPE_DIMS))
    cos_t = np.concatenate([cos, cos, ones], axis=1)
    sin_t = np.concatenate([-sin, sin, zeros_rest], axis=1)
    return jnp.asarray(cos_t, F32), jnp.asarray(sin_t, F32)


def kernel(x, mix_norm_g, w_in, moba_out_g, sb_out_g, w_out, mlp_norm_g, w_up, w_down, final_norm_g):
    batch, seq, d_model = x.shape
    depth = w_in.shape[0]
    cos_t, sin_t = _rope_tables(seq)
    x2 = x.reshape(batch * seq, d_model)
    for l in range(depth):
        qkv = _qkv_proj(x2, mix_norm_g[l][None, :], w_in[l], cos_t, sin_t,
                        seq=seq, tm=1024, tn=1024, n_groups=2)
        o_a, w_up_bf, w_out_bf = _moba_attn(qkv, w_up[l], w_out[l], batch=batch, seq=seq,
                                            heads_per_step=ATTN_HEADS_PER_STEP)
        o_b, w_down_bf = _sb_attn(qkv, w_down[l], batch=batch, seq=seq, tile=256,
                                  heads_per_step=ATTN_HEADS_PER_STEP)
        x1 = _out_proj(o_a, o_b, moba_out_g[l][None, :], sb_out_g[l][None, :],
                       w_out_bf, x2, tm=512)
        last = l == depth - 1
        assert last, "kernel fuses the final RMSNorm into the last layer's MLP; DEPTH must be 1"
        x2 = _mlp(x1, mlp_norm_g[l][None, :], w_up_bf, w_down_bf,
                  final_norm_g[None, :], tm=1024, tf=1024)
    return x2.reshape(batch, seq, d_model)
```

```python
import functools
import math

import jax
import jax.numpy as jnp
import numpy as np
from jax import lax
from jax.experimental import pallas as pl
from jax.experimental.pallas import tpu as pltpu

HEAD_DIM = 128
N_HEADS_MOBA = 8
N_HEADS_SB = 8
MOBA_BLOCK = 256
MOBA_TOPK = 3
ROPE_THETA = 500000.0
ROPE_DIMS = HEAD_DIM // 4
EPS = 1e-6
NEG = -1e30

F32 = jnp.float32
BF16 = jnp.bfloat16
ATTN_OUT_DTYPE = BF16
ATTN_HEADS_PER_STEP = 2

_NT = (((1,), (1,)), ((), ()))

VMEM_LIMIT_BYTES = 56 * 1024 * 1024
LARGE_VMEM_LIMIT_BYTES = 62 * 1024 * 1024

Q_SCALE = HEAD_DIM ** -0.5 * math.log2(math.e)


def _rms_scale(x):
    return lax.rsqrt(jnp.mean(x * x, axis=-1, keepdims=True) + EPS)


def _emit_in_order(events):
    for _, _, thunk in sorted(events, key=lambda e: (e[0], e[1])):
        thunk()


def _qkv_kernel(x_ref, g_ref, w_ref, cos_ref, sin_ref, o_ref, h_ref):
    j = pl.program_id(1)
    i = pl.program_id(2)
    tm = x_ref.shape[0]
    rows = pl.ds(pl.multiple_of(i * tm, tm), tm)

    def project(h=None):
        h = h_ref[rows, :] if h is None else h
        return jnp.dot(h, w_ref[...].astype(BF16), preferred_element_type=F32)

    def rope(y, post_scale):
        n_heads = y.shape[1] // HEAD_DIM
        half = ROPE_DIMS // 2
        lane = lax.broadcasted_iota(jnp.int32, (y.shape[0], HEAD_DIM), 1)
        partner = jnp.where(lane < ROPE_DIMS, lane ^ half, lane)
        for hd in range(n_heads):
            t = y[:, hd * HEAD_DIM:(hd + 1) * HEAD_DIM]
            swapped = jnp.take_along_axis(t, partner, axis=1)
            r = t * cos_ref[...] + swapped * sin_ref[...]
            if post_scale is not None:
                r = r * post_scale
            o_ref[:, hd * HEAD_DIM:(hd + 1) * HEAD_DIM] = r.astype(o_ref.dtype)

    @pl.when(j == 0)
    def _():
        x = x_ref[...]
        h = ((x * _rms_scale(x)) * g_ref[...]).astype(BF16)
        h_ref[rows, :] = h
        rope(project(h), Q_SCALE)

    @pl.when(j == 1)
    def _():
        rope(project(), None)

    @pl.when(j == 3)
    def _():
        o_ref[...] = (project() * Q_SCALE).astype(o_ref.dtype)

    @pl.when((j == 2) | (j >= 4))
    def _():
        o_ref[...] = project().astype(o_ref.dtype)


def _side_cast_specs(side, n_steps, index_map):
    rows, cols = side.shape
    slab = rows // n_steps
    assert slab * n_steps == rows and slab % 16 == 0, (rows, n_steps)
    spec = pl.BlockSpec((slab, cols), index_map)
    return spec, spec, jax.ShapeDtypeStruct((rows, cols), BF16)


def _qkv_proj(x2, g, w, cos_t, sin_t, *, seq, tm, tn, n_groups):
    m, d = x2.shape
    n = w.shape[1]
    tiles = m // (tm * n_groups)
    pos_blocks = seq // tm
    n_rope_tiles = 2

    def x_tile(gr, j, i):
        return gr * tiles + jnp.where(j == 0, i, tiles - 1)

    tab_spec = pl.BlockSpec(
        (tm, HEAD_DIM), lambda gr, j, i: (jnp.where(j < n_rope_tiles, (gr * tiles + i) % pos_blocks, 0), 0))
    return pl.pallas_call(
        _qkv_kernel,
        out_shape=jax.ShapeDtypeStruct((m, n), BF16),
        grid=(n_groups, n // tn, tiles),
        in_specs=[
            pl.BlockSpec((tm, d), lambda gr, j, i: (x_tile(gr, j, i), 0)),
            pl.BlockSpec((1, d), lambda gr, j, i: (0, 0)),
            pl.BlockSpec((d, tn), lambda gr, j, i: (0, j)),
            tab_spec, tab_spec,
        ],
        out_specs=pl.BlockSpec((tm, tn), lambda gr, j, i: (gr * tiles + i, j)),
        scratch_shapes=[pltpu.VMEM((tiles * tm, d), BF16)],
        compiler_params=pltpu.CompilerParams(
            dimension_semantics=("arbitrary", "arbitrary", "arbitrary"),
            vmem_limit_bytes=LARGE_VMEM_LIMIT_BYTES),
        name="qkv_proj",
    )(x2, g, w, cos_t, sin_t)


_MOBA_LAG_MASK, _MOBA_LAG_EXP, _MOBA_LAG_PV = 3, 2, 2


def _moba_kernel(q_ref, k_ref, v_ref, side_ref, side2_ref, o_ref, side_bf_ref, side2_bf_ref, vt_ref,
                 *, seq, heads):
    blk = MOBA_BLOCK
    n_blk = seq // blk

    key_i = lax.broadcasted_iota(jnp.int32, (blk, blk), 0)
    qry_i = lax.broadcasted_iota(jnp.int32, (blk, blk), 1)
    causal = key_i <= qry_i
    blk_id = lax.broadcasted_iota(jnp.int32, (n_blk, blk), 0)

    def head_cols(hd):
        return slice(hd * HEAD_DIM, (hd + 1) * HEAD_DIM)

    blocks = [(hd, i, j) for hd in range(heads)
              for i in range(n_blk - 1, -1, -1) for j in ([i] + list(range(i)))]
    last_of_tile = {i: (i - 1 if i else 0) for i in range(n_blk)}
    km_rows, raw, gate_raw, bias, scores, col_max, p_bf, acc, row_sum = {}, {}, {}, {}, {}, {}, {}, {}, {}

    def mean_keys(hd):
        kf = k_ref[:, head_cols(hd)].astype(F32)
        k_mean = jnp.concatenate(
            [jnp.mean(kf[n * blk:(n + 1) * blk, :], axis=0, keepdims=True) for n in range(n_blk)], axis=0)
        km_hi = k_mean.astype(BF16)
        km_lo = (k_mean - km_hi.astype(F32)).astype(BF16)
        km_rows[hd] = jnp.concatenate([km_hi, km_lo], axis=0)

    def transpose_values(hd):
        vt_ref[hd] = v_ref[:, head_cols(hd)].astype(F32).T.astype(BF16)

    def score_matmul(nb):
        hd, i, j = blocks[nb]
        q_i = q_ref[i * blk:(i + 1) * blk, head_cols(hd)]
        k_j = k_ref[j * blk:(j + 1) * blk, head_cols(hd)]
        if j == i:
            r = lax.dot_general(jnp.concatenate([k_j, km_rows[hd]], axis=0), q_i, _NT,
                                preferred_element_type=F32)
            raw[nb] = r[0:blk, :]
            gate_raw[hd, i] = r[blk:blk + n_blk, :] + r[blk + n_blk:blk + 2 * n_blk, :]
        else:
            raw[nb] = lax.dot_general(k_j, q_i, _NT, preferred_element_type=F32)

    def select_blocks(hd, i):
        past = blk_id < i
        g = jnp.where(past, gate_raw.pop((hd, i)), NEG)
        rank = jnp.zeros((n_blk, blk), jnp.int32)
        for other in range(n_blk):
            g_o = g[other:other + 1, :]
            beats = (g_o > g) | ((g_o == g) & (other < blk_id))
            rank = rank + beats.astype(jnp.int32)
        bias[hd, i] = jnp.where(past & (rank < MOBA_TOPK), 0.0, NEG).astype(F32)

    def mask_and_max(nb):
        hd, i, j = blocks[nb]
        s = raw.pop(nb)
        if j == i:
            s = jnp.where(causal, s, NEG)
            select_blocks(hd, i)
        else:
            s = s + bias[hd, i][j:j + 1, :]
        scores[nb] = s
        cm = jnp.max(s, axis=0, keepdims=True)
        col_max[hd, i] = cm if j == i else jnp.maximum(col_max[hd, i], cm)

    def exponentiate(nb):
        hd, i, j = blocks[nb]
        p = jnp.exp2(scores.pop(nb) - col_max[hd, i])
        ps = jnp.sum(p, axis=0, keepdims=True)
        row_sum[hd, i] = ps if j == i else row_sum[hd, i] + ps
        p_bf[nb] = p.astype(BF16)

    def value_matmul(nb):
        hd, i, j = blocks[nb]
        d = jnp.dot(vt_ref[hd, :, j * blk:(j + 1) * blk], p_bf.pop(nb), preferred_element_type=F32)
        acc[hd, i] = d if j == i else acc[hd, i] + d
        if j == last_of_tile[i]:
            out = (acc.pop((hd, i)) / row_sum.pop((hd, i))).T
            o_ref[i * blk:(i + 1) * blk, head_cols(hd)] = out.astype(o_ref.dtype)

    def cast_side():
        side_bf_ref[...] = side_ref[...].astype(BF16)
        side2_bf_ref[...] = side2_ref[...].astype(BF16)

    mask_step = [nb + _MOBA_LAG_MASK for nb in range(len(blocks))]
    tile_done = {}
    for nb, (hd, i, _) in enumerate(blocks):
        tile_done[hd, i] = max(tile_done.get((hd, i), 0), mask_step[nb])
    per_head = len(blocks) // heads
    events = [(2, 1, cast_side)]
    for hd in range(heads):
        first = hd * per_head
        events += [(max(first - _MOBA_LAG_MASK, 0), -1, functools.partial(mean_keys, hd)),
                   (first + 1, 1, functools.partial(transpose_values, hd))]
    exp_step = -1
    for nb, (hd, i, _) in enumerate(blocks):
        exp_step = max(exp_step + 1, tile_done[hd, i] + _MOBA_LAG_EXP)
        events += [
            (nb, 0, functools.partial(score_matmul, nb)),
            (exp_step + _MOBA_LAG_PV, 2, functools.partial(value_matmul, nb)),
            (exp_step, 3, functools.partial(exponentiate, nb)),
            (mask_step[nb], 4, functools.partial(mask_and_max, nb)),
        ]
    _emit_in_order(events)


def _moba_attn(qkv, side, side2, *, batch, seq, heads_per_step):
    kern = functools.partial(_moba_kernel, seq=seq, heads=heads_per_step)
    groups = N_HEADS_MOBA // heads_per_step
    hq, hk, hv = 0, groups, 2 * groups
    blk = (seq, heads_per_step * HEAD_DIM)
    slab_of_step = lambda b, h: (b * groups + h, 0)
    side_in, side_out, side_shape = _side_cast_specs(side, batch * groups, slab_of_step)
    side2_in, side2_out, side2_shape = _side_cast_specs(side2, batch * groups, slab_of_step)
    return pl.pallas_call(
        kern,
        out_shape=(jax.ShapeDtypeStruct((batch * seq, N_HEADS_MOBA * HEAD_DIM), ATTN_OUT_DTYPE),
                   side_shape, side2_shape),
        grid=(batch, groups),
        in_specs=[
            pl.BlockSpec(blk, lambda b, h: (b, hq + h)),
            pl.BlockSpec(blk, lambda b, h: (b, hk + h)),
            pl.BlockSpec(blk, lambda b, h: (b, hv + h)),
            side_in, side2_in,
        ],
        out_specs=(pl.BlockSpec(blk, lambda b, h: (b, h)), side_out, side2_out),
        scratch_shapes=[pltpu.VMEM((heads_per_step, HEAD_DIM, seq), BF16)],
        compiler_params=pltpu.CompilerParams(
            dimension_semantics=("arbitrary", "arbitrary"),
            vmem_limit_bytes=VMEM_LIMIT_BYTES),
        name="moba_attn",
    )(qkv, qkv, qkv, side, side2)


_SB_LAGS = (1, 1, 2, 1)


def _sb_kernel(q_ref, k_ref, v_ref, side_ref, o_ref, side_bf_ref, vt_ref, *, seq, tile, heads):
    n_tiles = seq // tile
    key_i = lax.broadcasted_iota(jnp.int32, (tile, tile), 0)
    qry_i = lax.broadcasted_iota(jnp.int32, (tile, tile), 1)
    causal = key_i < qry_i
    this_or_later = (qry_i >= key_i).astype(BF16)

    def head_cols(hd):
        return slice(hd * HEAD_DIM, (hd + 1) * HEAD_DIM)

    blocks = [(hd, i, j) for hd in range(heads)
              for i in range(n_tiles - 1, -1, -1) for j in range(i, -1, -1)]
    raw, logit, soft_bf, later_sum, a_bf, acc, carry = {}, {}, {}, {}, {}, {}, {}

    def logit_matmul(nb):
        hd, i, j = blocks[nb]
        raw[nb] = lax.dot_general(k_ref[j * tile:(j + 1) * tile, head_cols(hd)],
                                  q_ref[i * tile:(i + 1) * tile, head_cols(hd)],
                                  _NT, preferred_element_type=F32)

    def softplus(nb):
        _, i, j = blocks[nb]
        z = raw.pop(nb)
        t = jnp.maximum(z, 0.0) + jnp.log2(1.0 + jnp.exp2(-jnp.abs(z)))
        if j == i:
            t = jnp.where(causal, t, 0.0)
        logit[nb] = z
        soft_bf[nb] = t.astype(BF16)

    def cumsum_matmul(nb):
        later_sum[nb] = jnp.dot(this_or_later, soft_bf.pop(nb), preferred_element_type=F32)

    def weights(nb):
        hd, i, j = blocks[nb]
        inc = later_sum.pop(nb)
        x = logit.pop(nb) - inc
        if j != i:
            x = x - carry[hd, i]
        a = jnp.exp2(x)
        if j == i:
            a = jnp.where(causal, a, 0.0)
        total = inc[0:1, :]
        carry[hd, i] = total if j == i else carry[hd, i] + total
        a_bf[nb] = a.astype(BF16)

    def value_matmul(nb):
        hd, i, j = blocks[nb]
        d = jnp.dot(vt_ref[hd, :, j * tile:(j + 1) * tile], a_bf.pop(nb), preferred_element_type=F32)
        acc[hd, i] = d if j == i else acc[hd, i] + d
        if j == 0:
            o_ref[i * tile:(i + 1) * tile, head_cols(hd)] = acc.pop((hd, i)).T.astype(o_ref.dtype)

    def transpose_values(hd):
        vt_ref[hd] = v_ref[:, head_cols(hd)].astype(F32).T.astype(BF16)

    def cast_side():
        side_bf_ref[...] = side_ref[...].astype(BF16)

    l_soft, l_cum, l_w, l_pv = _SB_LAGS
    per_head = len(blocks) // heads
    events = [(2, 1, cast_side)]
    events += [(hd * per_head + 1, 1, functools.partial(transpose_values, hd)) for hd in range(heads)]
    for nb in range(len(blocks)):
        events += [
            (nb, 0, functools.partial(logit_matmul, nb)),
            (nb + l_soft + l_cum + l_w + l_pv, 2, functools.partial(value_matmul, nb)),
            (nb + l_soft + l_cum, 3, functools.partial(cumsum_matmul, nb)),
            (nb + l_soft + l_cum + l_w, 4, functools.partial(weights, nb)),
            (nb + l_soft, 5, functools.partial(softplus, nb)),
        ]
    _emit_in_order(events)


def _sb_attn(qkv, side, *, batch, seq, tile, heads_per_step):
    kern = functools.partial(_sb_kernel, seq=seq, tile=tile, heads=heads_per_step)
    groups = N_HEADS_SB // heads_per_step
    base = 3 * (N_HEADS_MOBA // heads_per_step)
    hq, hk, hv = base, base + groups, base + 2 * groups
    blk = (seq, heads_per_step * HEAD_DIM)
    side_in, side_out, side_shape = _side_cast_specs(
        side, batch * groups, lambda b, h: (b * groups + h, 0))
    return pl.pallas_call(
        kern,
        out_shape=(jax.ShapeDtypeStruct((batch * seq, N_HEADS_SB * HEAD_DIM), ATTN_OUT_DTYPE), side_shape),
        grid=(batch, groups),
        in_specs=[
            pl.BlockSpec(blk, lambda b, h: (b, hq + h)),
            pl.BlockSpec(blk, lambda b, h: (b, hk + h)),
            pl.BlockSpec(blk, lambda b, h: (b, hv + h)),
            side_in,
        ],
        out_specs=(pl.BlockSpec(blk, lambda b, h: (b, h)), side_out),
        scratch_shapes=[pltpu.VMEM((heads_per_step, HEAD_DIM, seq), BF16)],
        compiler_params=pltpu.CompilerParams(
            dimension_semantics=("arbitrary", "arbitrary"),
            vmem_limit_bytes=VMEM_LIMIT_BYTES),
        name="sb_attn",
    )(qkv, qkv, qkv, side)


def _out_proj_kernel(oa_ref, ob_ref, ga_ref, gb_ref, w_ref, x_ref, y_ref):
    d_a = oa_ref.shape[1]
    oa = oa_ref[...].astype(F32)
    ob = ob_ref[...].astype(F32)
    na = ((oa * _rms_scale(oa)) * ga_ref[...]).astype(BF16)
    nb = ((ob * _rms_scale(ob)) * gb_ref[...]).astype(BF16)
    n_chunk = 4
    cw = y_ref.shape[1] // n_chunk
    for c in range(n_chunk):
        cols = slice(c * cw, (c + 1) * cw)
        y = (jnp.dot(na, w_ref[0:d_a, cols], preferred_element_type=F32)
             + jnp.dot(nb, w_ref[d_a:, cols], preferred_element_type=F32))
        y_ref[:, cols] = x_ref[:, cols] + y


def _out_proj(o_a, o_b, g_a, g_b, w_bf16, x2, *, tm):
    m, d_a = o_a.shape
    d_b = o_b.shape[1]
    d = w_bf16.shape[1]
    return pl.pallas_call(
        _out_proj_kernel,
        out_shape=jax.ShapeDtypeStruct((m, d), F32),
        grid=(m // tm,),
        in_specs=[
            pl.BlockSpec((tm, d_a), lambda i: (i, 0)),
            pl.BlockSpec((tm, d_b), lambda i: (i, 0)),
            pl.BlockSpec((1, d_a), lambda i: (0, 0)),
            pl.BlockSpec((1, d_b), lambda i: (0, 0)),
            pl.BlockSpec((d_a + d_b, d), lambda i: (0, 0), pipeline_mode=pl.Buffered(1)),
            pl.BlockSpec((tm, d), lambda i: (i, 0)),
        ],
        out_specs=pl.BlockSpec((tm, d), lambda i: (i, 0)),
        compiler_params=pltpu.CompilerParams(
            dimension_semantics=("arbitrary",),
            vmem_limit_bytes=LARGE_VMEM_LIMIT_BYTES),
        name="out_proj",
    )(o_a, o_b, g_a, g_b, w_bf16, x2)


def _mlp_kernel(x_ref, g_ref, wu_ref, wd_ref, gf_ref, o_ref, h_ref, *, n_chunk):
    f = pl.program_id(1)
    n_f = pl.num_programs(1)

    cw = o_ref.shape[1] // n_chunk

    def mlp_slice(h, base):
        u = jnp.dot(h, wu_ref[...], preferred_element_type=F32)
        r = jnp.maximum(u, 0.0)
        act = (r * r).astype(BF16)
        for c in range(n_chunk):
            cols = slice(c * cw, (c + 1) * cw)
            o_ref[:, cols] = base(cols) + jnp.dot(act, wd_ref[:, cols], preferred_element_type=F32)

    @pl.when(f == 0)
    def _():
        x = x_ref[...]
        h = ((x * _rms_scale(x)) * g_ref[...]).astype(BF16)
        h_ref[...] = h
        mlp_slice(h, lambda cols: x_ref[:, cols])

    @pl.when(f > 0)
    def _():
        mlp_slice(h_ref[...], lambda cols: o_ref[:, cols])

    @pl.when(f == n_f - 1)
    def _():
        y = o_ref[...]
        o_ref[...] = (y * _rms_scale(y)) * gf_ref[...]


def _mlp(x1, g, wu_bf16, wd_bf16, g_final, *, tm, tf):
    m, d = x1.shape
    d_ff = wu_bf16.shape[1]
    kern = functools.partial(_mlp_kernel, n_chunk=4)
    return pl.pallas_call(
        kern,
        out_shape=jax.ShapeDtypeStruct((m, d), F32),
        grid=(m // tm, d_ff // tf),
        in_specs=[
            pl.BlockSpec((tm, d), lambda i, f: (i, 0)),
            pl.BlockSpec((1, d), lambda i, f: (0, 0)),
            pl.BlockSpec((d, tf), lambda i, f: (0, f)),
            pl.BlockSpec((tf, d), lambda i, f: (f, 0)),
            pl.BlockSpec((1, d), lambda i, f: (0, 0)),
        ],
        out_specs=pl.BlockSpec((tm, d), lambda i, f: (i, 0)),
        scratch_shapes=[pltpu.VMEM((tm, d), BF16)],
        compiler_params=pltpu.CompilerParams(
            dimension_semantics=("arbitrary", "arbitrary"),
            vmem_limit_bytes=LARGE_VMEM_LIMIT_BYTES),
        name="mlp",
    )(x1, g, wu_bf16, wd_bf16, g_final)


def _rope_tables(seq):
    half = ROPE_DIMS // 2
    inv_freq = ROPE_THETA ** (-np.arange(half, dtype=np.float64) / half)
    ang = np.arange(seq, dtype=np.float64)[:, None] * inv_freq[None, :]
    cos, sin = np.cos(ang), np.sin(ang)
    ones = np.ones((seq, HEAD_DIM - ROPE_DIMS))
    zeros_rest = np.zeros((seq, HEAD_DIM - ROPE_DIMS))
    cos_t = np.concatenate([cos, cos, ones], axis=1)
    sin_t = np.concatenate([-sin, sin, zeros_rest], axis=1)
    return jnp.asarray(cos_t, F32), jnp.asarray(sin_t, F32)


def kernel(x, mix_norm_g, w_in, moba_out_g, sb_out_g, w_out, mlp_norm_g, w_up, w_down, final_norm_g):
    batch, seq, d_model = x.shape
    depth = w_in.shape[0]
    cos_t, sin_t = _rope_tables(seq)
    x2 = x.reshape(batch * seq, d_model)
    for l in range(depth):
        qkv = _qkv_proj(x2, mix_norm_g[l][None, :], w_in[l], cos_t, sin_t,
                        seq=seq, tm=1024, tn=1024, n_groups=2)
        o_a, w_up_bf, w_out_bf = _moba_attn(qkv, w_up[l], w_out[l], batch=batch, seq=seq,
                                            heads_per_step=ATTN_HEADS_PER_STEP)
        o_b, w_down_bf = _sb_attn(qkv, w_down[l], batch=batch, seq=seq, tile=256,
                                  heads_per_step=ATTN_HEADS_PER_STEP)
        x1 = _out_proj(o_a, o_b, moba_out_g[l][None, :], sb_out_g[l][None, :],
                       w_out_bf, x2, tm=1024)
        last = l == depth - 1
        assert last, "kernel fuses the final RMSNorm into the last layer's MLP; DEPTH must be 1"
        x2 = _mlp(x1, mlp_norm_g[l][None, :], w_up_bf, w_down_bf,
                  final_norm_g[None, :], tm=1024, tf=1024)
    return x2.reshape(batch, seq, d_model)
```

```python
import functools
import math

import jax
import jax.numpy as jnp
import numpy as np
from jax import lax
from jax.experimental import pallas as pl
from jax.experimental.pallas import tpu as pltpu

HEAD_DIM = 128
N_HEADS_MOBA = 8
N_HEADS_SB = 8
MOBA_BLOCK = 256
MOBA_TOPK = 3
ROPE_THETA = 500000.0
ROPE_DIMS = HEAD_DIM // 4
EPS = 1e-6
NEG = -1e30

F32 = jnp.float32
BF16 = jnp.bfloat16
ATTN_OUT_DTYPE = BF16
ATTN_HEADS_PER_STEP = 2

_NT = (((1,), (1,)), ((), ()))

VMEM_LIMIT_BYTES = 56 * 1024 * 1024
LARGE_VMEM_LIMIT_BYTES = 62 * 1024 * 1024

Q_SCALE = HEAD_DIM ** -0.5 * math.log2(math.e)


def _rms_scale(x):
    return lax.rsqrt(jnp.mean(x * x, axis=-1, keepdims=True) + EPS)


def _emit_in_order(events):
    for _, _, thunk in sorted(events, key=lambda e: (e[0], e[1])):
        thunk()


def _qkv_kernel(x_ref, g_ref, w_ref, cos_ref, sin_ref, o_ref, h_ref):
    j = pl.program_id(1)
    i = pl.program_id(2)
    tm = x_ref.shape[0]
    rows = pl.ds(pl.multiple_of(i * tm, tm), tm)

    def project(h=None):
        h = h_ref[rows, :] if h is None else h
        return jnp.dot(h, w_ref[...].astype(BF16), preferred_element_type=F32)

    def rope(y, post_scale):
        n_heads = y.shape[1] // HEAD_DIM
        half = ROPE_DIMS // 2
        lane = lax.broadcasted_iota(jnp.int32, (y.shape[0], HEAD_DIM), 1)
        partner = jnp.where(lane < ROPE_DIMS, lane ^ half, lane)
        for hd in range(n_heads):
            t = y[:, hd * HEAD_DIM:(hd + 1) * HEAD_DIM]
            swapped = jnp.take_along_axis(t, partner, axis=1)
            r = t * cos_ref[...] + swapped * sin_ref[...]
            if post_scale is not None:
                r = r * post_scale
            o_ref[:, hd * HEAD_DIM:(hd + 1) * HEAD_DIM] = r.astype(o_ref.dtype)

    @pl.when(j == 0)
    def _():
        x = x_ref[...]
        h = ((x * _rms_scale(x)) * g_ref[...]).astype(BF16)
        h_ref[rows, :] = h
        rope(project(h), Q_SCALE)

    @pl.when(j == 1)
    def _():
        rope(project(), None)

    @pl.when(j == 3)
    def _():
        o_ref[...] = (project() * Q_SCALE).astype(o_ref.dtype)

    @pl.when((j == 2) | (j >= 4))
    def _():
        o_ref[...] = project().astype(o_ref.dtype)


def _side_cast_specs(side, n_steps, index_map):
    rows, cols = side.shape
    slab = rows // n_steps
    assert slab * n_steps == rows and slab % 16 == 0, (rows, n_steps)
    spec = pl.BlockSpec((slab, cols), index_map)
    return spec, spec, jax.ShapeDtypeStruct((rows, cols), BF16)


def _qkv_proj(x2, g, w, cos_t, sin_t, *, seq, tm, tn, n_groups):
    m, d = x2.shape
    n = w.shape[1]
    tiles = m // (tm * n_groups)
    pos_blocks = seq // tm
    n_rope_tiles = 2

    def x_tile(gr, j, i):
        return gr * tiles + jnp.where(j == 0, i, tiles - 1)

    tab_spec = pl.BlockSpec(
        (tm, HEAD_DIM), lambda gr, j, i: (jnp.where(j < n_rope_tiles, (gr * tiles + i) % pos_blocks, 0), 0))
    return pl.pallas_call(
        _qkv_kernel,
        out_shape=jax.ShapeDtypeStruct((m, n), BF16),
        grid=(n_groups, n // tn, tiles),
        in_specs=[
            pl.BlockSpec((tm, d), lambda gr, j, i: (x_tile(gr, j, i), 0)),
            pl.BlockSpec((1, d), lambda gr, j, i: (0, 0)),
            pl.BlockSpec((d, tn), lambda gr, j, i: (0, j)),
            tab_spec, tab_spec,
        ],
        out_specs=pl.BlockSpec((tm, tn), lambda gr, j, i: (gr * tiles + i, j)),
        scratch_shapes=[pltpu.VMEM((tiles * tm, d), BF16)],
        compiler_params=pltpu.CompilerParams(
            dimension_semantics=("arbitrary", "arbitrary", "arbitrary"),
            vmem_limit_bytes=LARGE_VMEM_LIMIT_BYTES),
        name="qkv_proj",
    )(x2, g, w, cos_t, sin_t)


_MOBA_LAG_MASK, _MOBA_LAG_EXP, _MOBA_LAG_PV = 3, 2, 2


def _moba_kernel(q_ref, k_ref, v_ref, side_ref, side2_ref, o_ref, side_bf_ref, side2_bf_ref, vt_ref,
                 *, seq, heads):
    blk = MOBA_BLOCK
    n_blk = seq // blk

    key_i = lax.broadcasted_iota(jnp.int32, (blk, blk), 0)
    qry_i = lax.broadcasted_iota(jnp.int32, (blk, blk), 1)
    causal = key_i <= qry_i
    blk_id = lax.broadcasted_iota(jnp.int32, (n_blk, blk), 0)

    def head_cols(hd):
        return slice(hd * HEAD_DIM, (hd + 1) * HEAD_DIM)

    blocks = [(hd, i, j) for hd in range(heads)
              for i in range(n_blk - 1, -1, -1) for j in ([i] + list(range(i)))]
    last_of_tile = {i: (i - 1 if i else 0) for i in range(n_blk)}
    km_rows, raw, gate_raw, bias, scores, col_max, p_bf, acc, row_sum = {}, {}, {}, {}, {}, {}, {}, {}, {}

    def mean_keys(hd):
        kf = k_ref[:, head_cols(hd)].astype(F32)
        k_mean = jnp.concatenate(
            [jnp.mean(kf[n * blk:(n + 1) * blk, :], axis=0, keepdims=True) for n in range(n_blk)], axis=0)
        km_hi = k_mean.astype(BF16)
        km_lo = (k_mean - km_hi.astype(F32)).astype(BF16)
        km_rows[hd] = jnp.concatenate([km_hi, km_lo], axis=0)

    def transpose_values(hd):
        vt_ref[hd] = v_ref[:, head_cols(hd)].astype(F32).T.astype(BF16)

    def score_matmul(nb):
        hd, i, j = blocks[nb]
        q_i = q_ref[i * blk:(i + 1) * blk, head_cols(hd)]
        k_j = k_ref[j * blk:(j + 1) * blk, head_cols(hd)]
        if j == i:
            r = lax.dot_general(jnp.concatenate([k_j, km_rows[hd]], axis=0), q_i, _NT,
                                preferred_element_type=F32)
            raw[nb] = r[0:blk, :]
            gate_raw[hd, i] = r[blk:blk + n_blk, :] + r[blk + n_blk:blk + 2 * n_blk, :]
        else:
            raw[nb] = lax.dot_general(k_j, q_i, _NT, preferred_element_type=F32)

    def select_blocks(hd, i):
        past = blk_id < i
        g = jnp.where(past, gate_raw.pop((hd, i)), NEG)
        rank = jnp.zeros((n_blk, blk), jnp.int32)
        for other in range(n_blk):
            g_o = g[other:other + 1, :]
            beats = (g_o > g) | ((g_o == g) & (other < blk_id))
            rank = rank + beats.astype(jnp.int32)
        bias[hd, i] = jnp.where(past & (rank < MOBA_TOPK), 0.0, NEG).astype(F32)

    def mask_and_max(nb):
        hd, i, j = blocks[nb]
        s = raw.pop(nb)
        if j == i:
            s = jnp.where(causal, s, NEG)
            select_blocks(hd, i)
        else:
            s = s + bias[hd, i][j:j + 1, :]
        scores[nb] = s
        cm = jnp.max(s, axis=0, keepdims=True)
        col_max[hd, i] = cm if j == i else jnp.maximum(col_max[hd, i], cm)

    def exponentiate(nb):
        hd, i, j = blocks[nb]
        p = jnp.exp2(scores.pop(nb) - col_max[hd, i])
        ps = jnp.sum(p, axis=0, keepdims=True)
        row_sum[hd, i] = ps if j == i else row_sum[hd, i] + ps
        p_bf[nb] = p.astype(BF16)

    def value_matmul(nb):
        hd, i, j = blocks[nb]
        d = jnp.dot(vt_ref[hd, :, j * blk:(j + 1) * blk], p_bf.pop(nb), preferred_element_type=F32)
        acc[hd, i] = d if j == i else acc[hd, i] + d
        if j == last_of_tile[i]:
            out = (acc.pop((hd, i)) / row_sum.pop((hd, i))).T
            o_ref[i * blk:(i + 1) * blk, head_cols(hd)] = out.astype(o_ref.dtype)

    def cast_side():
        side_bf_ref[...] = side_ref[...].astype(BF16)
        side2_bf_ref[...] = side2_ref[...].astype(BF16)

    mask_step = [nb + _MOBA_LAG_MASK for nb in range(len(blocks))]
    tile_done = {}
    for nb, (hd, i, _) in enumerate(blocks):
        tile_done[hd, i] = max(tile_done.get((hd, i), 0), mask_step[nb])
    per_head = len(blocks) // heads
    events = [(2, 1, cast_side)]
    for hd in range(heads):
        first = hd * per_head
        events += [(max(first - _MOBA_LAG_MASK, 0), -1, functools.partial(mean_keys, hd)),
                   (first + 1, 1, functools.partial(transpose_values, hd))]
    exp_step = -1
    for nb, (hd, i, _) in enumerate(blocks):
        exp_step = max(exp_step + 1, tile_done[hd, i] + _MOBA_LAG_EXP)
        events += [
            (nb, 0, functools.partial(score_matmul, nb)),
            (exp_step + _MOBA_LAG_PV, 2, functools.partial(value_matmul, nb)),
            (exp_step, 3, functools.partial(exponentiate, nb)),
            (mask_step[nb], 4, functools.partial(mask_and_max, nb)),
        ]
    _emit_in_order(events)


def _moba_attn(qkv, side, side2, *, batch, seq, heads_per_step):
    kern = functools.partial(_moba_kernel, seq=seq, heads=heads_per_step)
    groups = N_HEADS_MOBA // heads_per_step
    hq, hk, hv = 0, groups, 2 * groups
    blk = (seq, heads_per_step * HEAD_DIM)
    slab_of_step = lambda b, h: (b * groups + h, 0)
    side_in, side_out, side_shape = _side_cast_specs(side, batch * groups, slab_of_step)
    side2_in, side2_out, side2_shape = _side_cast_specs(side2, batch * groups, slab_of_step)
    return pl.pallas_call(
        kern,
        out_shape=(jax.ShapeDtypeStruct((batch * seq, N_HEADS_MOBA * HEAD_DIM), ATTN_OUT_DTYPE),
                   side_shape, side2_shape),
        grid=(batch, groups),
        in_specs=[
            pl.BlockSpec(blk, lambda b, h: (b, hq + h)),
            pl.BlockSpec(blk, lambda b, h: (b, hk + h)),
            pl.BlockSpec(blk, lambda b, h: (b, hv + h)),
            side_in, side2_in,
        ],
        out_specs=(pl.BlockSpec(blk, lambda b, h: (b, h)), side_out, side2_out),
        scratch_shapes=[pltpu.VMEM((heads_per_step, HEAD_DIM, seq), BF16)],
        compiler_params=pltpu.CompilerParams(
            dimension_semantics=("arbitrary", "arbitrary"),
            vmem_limit_bytes=VMEM_LIMIT_BYTES),
        name="moba_attn",
    )(qkv, qkv, qkv, side, side2)


_SB_LAGS = (1, 1, 2, 1)


def _sb_kernel(q_ref, k_ref, v_ref, side_ref, o_ref, side_bf_ref, vt_ref, *, seq, tile, heads):
    n_tiles = seq // tile
    key_i = lax.broadcasted_iota(jnp.int32, (tile, tile), 0)
    qry_i = lax.broadcasted_iota(jnp.int32, (tile, tile), 1)
    causal = key_i < qry_i
    this_or_later = (qry_i >= key_i).astype(BF16)

    def head_cols(hd):
        return slice(hd * HEAD_DIM, (hd + 1) * HEAD_DIM)

    blocks = [(hd, i, j) for hd in range(heads)
              for i in range(n_tiles - 1, -1, -1) for j in range(i, -1, -1)]
    raw, logit, soft_bf, later_sum, a_bf, acc, carry = {}, {}, {}, {}, {}, {}, {}

    def logit_matmul(nb):
        hd, i, j = blocks[nb]
        raw[nb] = lax.dot_general(k_ref[j * tile:(j + 1) * tile, head_cols(hd)],
                                  q_ref[i * tile:(i + 1) * tile, head_cols(hd)],
                                  _NT, preferred_element_type=F32)

    def softplus(nb):
        _, i, j = blocks[nb]
        z = raw.pop(nb)
        zb = z.astype(BF16)
        t = jnp.maximum(zb, 0.0) + jnp.log(1.0 + jnp.exp2(-jnp.abs(zb))) * math.log2(math.e)
        if j == i:
            t = jnp.where(causal, t, 0.0)
        logit[nb] = z
        soft_bf[nb] = t.astype(BF16)

    def cumsum_matmul(nb):
        later_sum[nb] = jnp.dot(this_or_later, soft_bf.pop(nb), preferred_element_type=F32)

    def weights(nb):
        hd, i, j = blocks[nb]
        inc = later_sum.pop(nb)
        x = logit.pop(nb) - inc
        if j != i:
            x = x - carry[hd, i]
        a = jnp.exp2(x)
        if j == i:
            a = jnp.where(causal, a, 0.0)
        total = inc[0:1, :]
        carry[hd, i] = total if j == i else carry[hd, i] + total
        a_bf[nb] = a.astype(BF16)

    def value_matmul(nb):
        hd, i, j = blocks[nb]
        d = jnp.dot(vt_ref[hd, :, j * tile:(j + 1) * tile], a_bf.pop(nb), preferred_element_type=F32)
        acc[hd, i] = d if j == i else acc[hd, i] + d
        if j == 0:
            o_ref[i * tile:(i + 1) * tile, head_cols(hd)] = acc.pop((hd, i)).T.astype(o_ref.dtype)

    def transpose_values(hd):
        vt_ref[hd] = v_ref[:, head_cols(hd)].astype(F32).T.astype(BF16)

    def cast_side():
        side_bf_ref[...] = side_ref[...].astype(BF16)

    l_soft, l_cum, l_w, l_pv = _SB_LAGS
    per_head = len(blocks) // heads
    events = [(2, 1, cast_side)]
    events += [(hd * per_head + 1, 1, functools.partial(transpose_values, hd)) for hd in range(heads)]
    for nb in range(len(blocks)):
        events += [
            (nb, 0, functools.partial(logit_matmul, nb)),
            (nb + l_soft + l_cum + l_w + l_pv, 2, functools.partial(value_matmul, nb)),
            (nb + l_soft + l_cum, 3, functools.partial(cumsum_matmul, nb)),
            (nb + l_soft + l_cum + l_w, 4, functools.partial(weights, nb)),
            (nb + l_soft, 5, functools.partial(softplus, nb)),
        ]
    _emit_in_order(events)


def _sb_attn(qkv, side, *, batch, seq, tile, heads_per_step):
    kern = functools.partial(_sb_kernel, seq=seq, tile=tile, heads=heads_per_step)
    groups = N_HEADS_SB // heads_per_step
    base = 3 * (N_HEADS_MOBA // heads_per_step)
    hq, hk, hv = base, base + groups, base + 2 * groups
    blk = (seq, heads_per_step * HEAD_DIM)
    side_in, side_out, side_shape = _side_cast_specs(
        side, batch * groups, lambda b, h: (b * groups + h, 0))
    return pl.pallas_call(
        kern,
        out_shape=(jax.ShapeDtypeStruct((batch * seq, N_HEADS_SB * HEAD_DIM), ATTN_OUT_DTYPE), side_shape),
        grid=(batch, groups),
        in_specs=[
            pl.BlockSpec(blk, lambda b, h: (b, hq + h)),
            pl.BlockSpec(blk, lambda b, h: (b, hk + h)),
            pl.BlockSpec(blk, lambda b, h: (b, hv + h)),
            side_in,
        ],
        out_specs=(pl.BlockSpec(blk, lambda b, h: (b, h)), side_out),
        scratch_shapes=[pltpu.VMEM((heads_per_step, HEAD_DIM, seq), BF16)],
        compiler_params=pltpu.CompilerParams(
            dimension_semantics=("arbitrary", "arbitrary"),
            vmem_limit_bytes=VMEM_LIMIT_BYTES),
        name="sb_attn",
    )(qkv, qkv, qkv, side)


def _out_proj_kernel(oa_ref, ob_ref, ga_ref, gb_ref, w_ref, x_ref, y_ref):
    d_a = oa_ref.shape[1]
    oa = oa_ref[...].astype(F32)
    ob = ob_ref[...].astype(F32)
    na = ((oa * _rms_scale(oa)) * ga_ref[...]).astype(BF16)
    nb = ((ob * _rms_scale(ob)) * gb_ref[...]).astype(BF16)
    n_chunk = 4
    cw = y_ref.shape[1] // n_chunk
    for c in range(n_chunk):
        cols = slice(c * cw, (c + 1) * cw)
        y = (jnp.dot(na, w_ref[0:d_a, cols], preferred_element_type=F32)
             + jnp.dot(nb, w_ref[d_a:, cols], preferred_element_type=F32))
        y_ref[:, cols] = x_ref[:, cols] + y


def _out_proj(o_a, o_b, g_a, g_b, w_bf16, x2, *, tm):
    m, d_a = o_a.shape
    d_b = o_b.shape[1]
    d = w_bf16.shape[1]
    return pl.pallas_call(
        _out_proj_kernel,
        out_shape=jax.ShapeDtypeStruct((m, d), F32),
        grid=(m // tm,),
        in_specs=[
            pl.BlockSpec((tm, d_a), lambda i: (i, 0)),
            pl.BlockSpec((tm, d_b), lambda i: (i, 0)),
            pl.BlockSpec((1, d_a), lambda i: (0, 0)),
            pl.BlockSpec((1, d_b), lambda i: (0, 0)),
            pl.BlockSpec((d_a + d_b, d), lambda i: (0, 0), pipeline_mode=pl.Buffered(1)),
            pl.BlockSpec((tm, d), lambda i: (i, 0)),
        ],
        out_specs=pl.BlockSpec((tm, d), lambda i: (i, 0)),
        compiler_params=pltpu.CompilerParams(
            dimension_semantics=("arbitrary",),
            vmem_limit_bytes=LARGE_VMEM_LIMIT_BYTES),
        name="out_proj",
    )(o_a, o_b, g_a, g_b, w_bf16, x2)


def _mlp_kernel(x_ref, g_ref, wu_ref, wd_ref, gf_ref, o_ref, h_ref, *, n_chunk):
    f = pl.program_id(1)
    n_f = pl.num_programs(1)

    cw = o_ref.shape[1] // n_chunk

    def mlp_slice(h, base):
        u = jnp.dot(h, wu_ref[...], preferred_element_type=F32)
        r = jnp.maximum(u, 0.0)
        act = (r * r).astype(BF16)
        for c in range(n_chunk):
            cols = slice(c * cw, (c + 1) * cw)
            o_ref[:, cols] = base(cols) + jnp.dot(act, wd_ref[:, cols], preferred_element_type=F32)

    @pl.when(f == 0)
    def _():
        x = x_ref[...]
        h = ((x * _rms_scale(x)) * g_ref[...]).astype(BF16)
        h_ref[...] = h
        mlp_slice(h, lambda cols: x_ref[:, cols])

    @pl.when(f > 0)
    def _():
        mlp_slice(h_ref[...], lambda cols: o_ref[:, cols])

    @pl.when(f == n_f - 1)
    def _():
        y = o_ref[...]
        o_ref[...] = (y * _rms_scale(y)) * gf_ref[...]


def _mlp(x1, g, wu_bf16, wd_bf16, g_final, *, tm, tf):
    m, d = x1.shape
    d_ff = wu_bf16.shape[1]
    kern = functools.partial(_mlp_kernel, n_chunk=4)
    return pl.pallas_call(
        kern,
        out_shape=jax.ShapeDtypeStruct((m, d), F32),
        grid=(m // tm, d_ff // tf),
        in_specs=[
            pl.BlockSpec((tm, d), lambda i, f: (i, 0)),
            pl.BlockSpec((1, d), lambda i, f: (0, 0)),
            pl.BlockSpec((d, tf), lambda i, f: (0, f)),
            pl.BlockSpec((tf, d), lambda i, f: (f, 0)),
            pl.BlockSpec((1, d), lambda i, f: (0, 0)),
        ],
        out_specs=pl.BlockSpec((tm, d), lambda i, f: (i, 0)),
        scratch_shapes=[pltpu.VMEM((tm, d), BF16)],
        compiler_params=pltpu.CompilerParams(
            dimension_semantics=("arbitrary", "arbitrary"),
            vmem_limit_bytes=LARGE_VMEM_LIMIT_BYTES),
        name="mlp",
    )(x1, g, wu_bf16, wd_bf16, g_final)


def _rope_tables(seq):
    half = ROPE_DIMS // 2
    inv_freq = ROPE_THETA ** (-np.arange(half, dtype=np.float64) / half)
    ang = np.arange(seq, dtype=np.float64)[:, None] * inv_freq[None, :]
    cos, sin = np.cos(ang), np.sin(ang)
    ones = np.ones((seq, HEAD_DIM - ROPE_DIMS))
    zeros_rest = np.zeros((seq, HEAD_DIM - ROPE_DIMS))
    cos_t = np.concatenate([cos, cos, ones], axis=1)
    sin_t = np.concatenate([-sin, sin, zeros_rest], axis=1)
    return jnp.asarray(cos_t, F32), jnp.asarray(sin_t, F32)


def kernel(x, mix_norm_g, w_in, moba_out_g, sb_out_g, w_out, mlp_norm_g, w_up, w_down, final_norm_g):
    batch, seq, d_model = x.shape
    depth = w_in.shape[0]
    cos_t, sin_t = _rope_tables(seq)
    x2 = x.reshape(batch * seq, d_model)
    for l in range(depth):
        qkv = _qkv_proj(x2, mix_norm_g[l][None, :], w_in[l], cos_t, sin_t,
                        seq=seq, tm=1024, tn=1024, n_groups=2)
        o_a, w_up_bf, w_out_bf = _moba_attn(qkv, w_up[l], w_out[l], batch=batch, seq=seq,
                                            heads_per_step=ATTN_HEADS_PER_STEP)
        o_b, w_down_bf = _sb_attn(qkv, w_down[l], batch=batch, seq=seq, tile=256,
                                  heads_per_step=ATTN_HEADS_PER_STEP)
        x1 = _out_proj(o_a, o_b, moba_out_g[l][None, :], sb_out_g[l][None, :],
                       w_out_bf, x2, tm=1024)
        last = l == depth - 1
        assert last, "kernel fuses the final RMSNorm into the last layer's MLP; DEPTH must be 1"
        x2 = _mlp(x1, mlp_norm_g[l][None, :], w_up_bf, w_down_bf,
                  final_norm_g[None, :], tm=1024, tf=1024)
    return x2.reshape(batch, seq, d_model)
```

```python
import functools
import math

import jax
import jax.numpy as jnp
import numpy as np
from jax import lax
from jax.experimental import pallas as pl
from jax.experimental.pallas import tpu as pltpu

HEAD_DIM = 128
N_HEADS_MOBA = 8
N_HEADS_SB = 8
MOBA_BLOCK = 256
MOBA_TOPK = 3
ROPE_THETA = 500000.0
ROPE_DIMS = HEAD_DIM // 4
EPS = 1e-6
NEG = -1e30

F32 = jnp.float32
BF16 = jnp.bfloat16
ATTN_OUT_DTYPE = BF16
ATTN_HEADS_PER_STEP = 2

_NT = (((1,), (1,)), ((), ()))

VMEM_LIMIT_BYTES = 56 * 1024 * 1024
LARGE_VMEM_LIMIT_BYTES = 62 * 1024 * 1024

Q_SCALE = HEAD_DIM ** -0.5 * math.log2(math.e)


def _rms_scale(x):
    return lax.rsqrt(jnp.mean(x * x, axis=-1, keepdims=True) + EPS)


def _emit_in_order(events):
    for _, _, thunk in sorted(events, key=lambda e: (e[0], e[1])):
        thunk()


def _qkv_kernel(x_ref, g_ref, w_ref, cos_ref, sin_ref, o_ref, h_ref):
    j = pl.program_id(1)
    i = pl.program_id(2)
    tm = x_ref.shape[0]
    rows = pl.ds(pl.multiple_of(i * tm, tm), tm)

    def project(h=None):
        h = h_ref[rows, :] if h is None else h
        return jnp.dot(h, w_ref[...].astype(BF16), preferred_element_type=F32)

    def rope(y, post_scale):
        n_heads = y.shape[1] // HEAD_DIM
        half = ROPE_DIMS // 2
        lane = lax.broadcasted_iota(jnp.int32, (y.shape[0], HEAD_DIM), 1)
        partner = jnp.where(lane < ROPE_DIMS, lane ^ half, lane)
        for hd in range(n_heads):
            t = y[:, hd * HEAD_DIM:(hd + 1) * HEAD_DIM]
            swapped = jnp.take_along_axis(t, partner, axis=1)
            r = t * cos_ref[...] + swapped * sin_ref[...]
            if post_scale is not None:
                r = r * post_scale
            o_ref[:, hd * HEAD_DIM:(hd + 1) * HEAD_DIM] = r.astype(o_ref.dtype)

    @pl.when(j == 0)
    def _():
        x = x_ref[...]
        h = ((x * _rms_scale(x)) * g_ref[...]).astype(BF16)
        h_ref[rows, :] = h
        rope(project(h), Q_SCALE)

    @pl.when(j == 1)
    def _():
        rope(project(), None)

    @pl.when(j == 3)
    def _():
        o_ref[...] = (project() * Q_SCALE).astype(o_ref.dtype)

    @pl.when((j == 2) | (j >= 4))
    def _():
        o_ref[...] = project().astype(o_ref.dtype)


def _side_cast_specs(side, n_steps, index_map):
    rows, cols = side.shape
    slab = rows // n_steps
    assert slab * n_steps == rows and slab % 16 == 0, (rows, n_steps)
    spec = pl.BlockSpec((slab, cols), index_map)
    return spec, spec, jax.ShapeDtypeStruct((rows, cols), BF16)


def _qkv_proj(x2, g, w, cos_t, sin_t, *, seq, tm, tn, n_groups):
    m, d = x2.shape
    n = w.shape[1]
    tiles = m // (tm * n_groups)
    pos_blocks = seq // tm
    n_rope_tiles = 2

    def x_tile(gr, j, i):
        return gr * tiles + jnp.where(j == 0, i, tiles - 1)

    tab_spec = pl.BlockSpec(
        (tm, HEAD_DIM), lambda gr, j, i: (jnp.where(j < n_rope_tiles, (gr * tiles + i) % pos_blocks, 0), 0))
    return pl.pallas_call(
        _qkv_kernel,
        out_shape=jax.ShapeDtypeStruct((m, n), BF16),
        grid=(n_groups, n // tn, tiles),
        in_specs=[
            pl.BlockSpec((tm, d), lambda gr, j, i: (x_tile(gr, j, i), 0)),
            pl.BlockSpec((1, d), lambda gr, j, i: (0, 0)),
            pl.BlockSpec((d, tn), lambda gr, j, i: (0, j)),
            tab_spec, tab_spec,
        ],
        out_specs=pl.BlockSpec((tm, tn), lambda gr, j, i: (gr * tiles + i, j)),
        scratch_shapes=[pltpu.VMEM((tiles * tm, d), BF16)],
        compiler_params=pltpu.CompilerParams(
            dimension_semantics=("arbitrary", "arbitrary", "arbitrary"),
            vmem_limit_bytes=LARGE_VMEM_LIMIT_BYTES),
        name="qkv_proj",
    )(x2, g, w, cos_t, sin_t)


_MOBA_LAG_MASK, _MOBA_LAG_EXP, _MOBA_LAG_PV = 3, 2, 2


def _moba_kernel(q_ref, k_ref, v_ref, side_ref, side2_ref, o_ref, side_bf_ref, side2_bf_ref, vt_ref,
                 *, seq, heads):
    blk = MOBA_BLOCK
    n_blk = seq // blk

    key_i = lax.broadcasted_iota(jnp.int32, (blk, blk), 0)
    qry_i = lax.broadcasted_iota(jnp.int32, (blk, blk), 1)
    causal = key_i <= qry_i
    blk_id = lax.broadcasted_iota(jnp.int32, (n_blk, blk), 0)

    def head_cols(hd):
        return slice(hd * HEAD_DIM, (hd + 1) * HEAD_DIM)

    blocks = [(hd, i, j) for hd in range(heads)
              for i in range(n_blk - 1, -1, -1) for j in ([i] + list(range(i)))]
    last_of_tile = {i: (i - 1 if i else 0) for i in range(n_blk)}
    km_rows, raw, gate_raw, bias, scores, col_max, p_bf, acc, row_sum = {}, {}, {}, {}, {}, {}, {}, {}, {}

    def mean_keys(hd):
        kf = k_ref[:, head_cols(hd)].astype(F32)
        k_mean = jnp.concatenate(
            [jnp.mean(kf[n * blk:(n + 1) * blk, :], axis=0, keepdims=True) for n in range(n_blk)], axis=0)
        km_hi = k_mean.astype(BF16)
        km_lo = (k_mean - km_hi.astype(F32)).astype(BF16)
        km_rows[hd] = jnp.concatenate([km_hi, km_lo], axis=0)

    def transpose_values(hd):
        vt_ref[hd] = v_ref[:, head_cols(hd)].astype(F32).T.astype(BF16)

    def score_matmul(nb):
        hd, i, j = blocks[nb]
        q_i = q_ref[i * blk:(i + 1) * blk, head_cols(hd)]
        k_j = k_ref[j * blk:(j + 1) * blk, head_cols(hd)]
        if j == i:
            r = lax.dot_general(jnp.concatenate([k_j, km_rows[hd]], axis=0), q_i, _NT,
                                preferred_element_type=F32)
            raw[nb] = r[0:blk, :]
            gate_raw[hd, i] = r[blk:blk + n_blk, :] + r[blk + n_blk:blk + 2 * n_blk, :]
        else:
            raw[nb] = lax.dot_general(k_j, q_i, _NT, preferred_element_type=F32)

    def select_blocks(hd, i):
        past = blk_id < i
        g = jnp.where(past, gate_raw.pop((hd, i)), NEG)
        rank = jnp.zeros((n_blk, blk), jnp.int32)
        for other in range(n_blk):
            g_o = g[other:other + 1, :]
            beats = (g_o > g) | ((g_o == g) & (other < blk_id))
            rank = rank + beats.astype(jnp.int32)
        bias[hd, i] = jnp.where(past & (rank < MOBA_TOPK), 0.0, NEG).astype(F32)

    def mask_and_max(nb):
        hd, i, j = blocks[nb]
        s = raw.pop(nb)
        if j == i:
            s = jnp.where(causal, s, NEG)
            select_blocks(hd, i)
        else:
            s = s + bias[hd, i][j:j + 1, :]
        scores[nb] = s
        cm = jnp.max(s, axis=0, keepdims=True)
        col_max[hd, i] = cm if j == i else jnp.maximum(col_max[hd, i], cm)

    def exponentiate(nb):
        hd, i, j = blocks[nb]
        p = jnp.exp2(scores.pop(nb) - col_max[hd, i])
        ps = jnp.sum(p, axis=0, keepdims=True)
        row_sum[hd, i] = ps if j == i else row_sum[hd, i] + ps
        p_bf[nb] = p.astype(BF16)

    def value_matmul(nb):
        hd, i, j = blocks[nb]
        d = jnp.dot(vt_ref[hd, :, j * blk:(j + 1) * blk], p_bf.pop(nb), preferred_element_type=F32)
        acc[hd, i] = d if j == i else acc[hd, i] + d
        if j == last_of_tile[i]:
            out = (acc.pop((hd, i)) / row_sum.pop((hd, i))).T
            o_ref[i * blk:(i + 1) * blk, head_cols(hd)] = out.astype(o_ref.dtype)

    def cast_side():
        side_bf_ref[...] = side_ref[...].astype(BF16)
        side2_bf_ref[...] = side2_ref[...].astype(BF16)

    mask_step = [nb + _MOBA_LAG_MASK for nb in range(len(blocks))]
    tile_done = {}
    for nb, (hd, i, _) in enumerate(blocks):
        tile_done[hd, i] = max(tile_done.get((hd, i), 0), mask_step[nb])
    per_head = len(blocks) // heads
    events = [(2, 1, cast_side)]
    for hd in range(heads):
        first = hd * per_head
        events += [(max(first - _MOBA_LAG_MASK, 0), -1, functools.partial(mean_keys, hd)),
                   (first + 1, 1, functools.partial(transpose_values, hd))]
    exp_step = -1
    for nb, (hd, i, _) in enumerate(blocks):
        exp_step = max(exp_step + 1, tile_done[hd, i] + _MOBA_LAG_EXP)
        events += [
            (nb, 0, functools.partial(score_matmul, nb)),
            (exp_step + _MOBA_LAG_PV, 2, functools.partial(value_matmul, nb)),
            (exp_step, 3, functools.partial(exponentiate, nb)),
            (mask_step[nb], 4, functools.partial(mask_and_max, nb)),
        ]
    _emit_in_order(events)


def _moba_attn(qkv, side, side2, *, batch, seq, heads_per_step):
    kern = functools.partial(_moba_kernel, seq=seq, heads=heads_per_step)
    groups = N_HEADS_MOBA // heads_per_step
    hq, hk, hv = 0, groups, 2 * groups
    blk = (seq, heads_per_step * HEAD_DIM)
    slab_of_step = lambda b, h: (b * groups + h, 0)
    side_in, side_out, side_shape = _side_cast_specs(side, batch * groups, slab_of_step)
    side2_in, side2_out, side2_shape = _side_cast_specs(side2, batch * groups, slab_of_step)
    return pl.pallas_call(
        kern,
        out_shape=(jax.ShapeDtypeStruct((batch * seq, N_HEADS_MOBA * HEAD_DIM), ATTN_OUT_DTYPE),
                   side_shape, side2_shape),
        grid=(batch, groups),
        in_specs=[
            pl.BlockSpec(blk, lambda b, h: (b, hq + h)),
            pl.BlockSpec(blk, lambda b, h: (b, hk + h)),
            pl.BlockSpec(blk, lambda b, h: (b, hv + h)),
            side_in, side2_in,
        ],
        out_specs=(pl.BlockSpec(blk, lambda b, h: (b, h)), side_out, side2_out),
        scratch_shapes=[pltpu.VMEM((heads_per_step, HEAD_DIM, seq), BF16)],
        compiler_params=pltpu.CompilerParams(
            dimension_semantics=("arbitrary", "arbitrary"),
            vmem_limit_bytes=VMEM_LIMIT_BYTES),
        name="moba_attn",
    )(qkv, qkv, qkv, side, side2)


_SB_LAGS = (1, 1, 2, 1)
_LOG2E_HI = float(np.asarray(math.log2(math.e), dtype=jnp.bfloat16))
_LOG2E_LO = float(np.asarray(math.log2(math.e) - _LOG2E_HI, dtype=jnp.bfloat16))


def _sb_kernel(q_ref, k_ref, v_ref, side_ref, o_ref, side_bf_ref, vt_ref, *, seq, tile, heads):
    n_tiles = seq // tile
    key_i = lax.broadcasted_iota(jnp.int32, (tile, tile), 0)
    qry_i = lax.broadcasted_iota(jnp.int32, (tile, tile), 1)
    causal = key_i < qry_i
    this_or_later = (qry_i >= key_i).astype(BF16)

    def head_cols(hd):
        return slice(hd * HEAD_DIM, (hd + 1) * HEAD_DIM)

    blocks = [(hd, i, j) for hd in range(heads)
              for i in range(n_tiles - 1, -1, -1) for j in range(i, -1, -1)]
    raw, logit, soft_bf, later_sum, a_bf, acc, carry = {}, {}, {}, {}, {}, {}, {}

    def logit_matmul(nb):
        hd, i, j = blocks[nb]
        raw[nb] = lax.dot_general(k_ref[j * tile:(j + 1) * tile, head_cols(hd)],
                                  q_ref[i * tile:(i + 1) * tile, head_cols(hd)],
                                  _NT, preferred_element_type=F32)

    def softplus(nb):
        _, i, j = blocks[nb]
        z = raw.pop(nb)
        zb = z.astype(BF16)
        ln_term = jnp.log(1.0 + jnp.exp2(-jnp.abs(zb)))
        t = jnp.maximum(zb, 0.0) + (ln_term * _LOG2E_HI + ln_term * _LOG2E_LO)
        if j == i:
            t = jnp.where(causal, t, 0.0)
        logit[nb] = z
        soft_bf[nb] = t.astype(BF16)

    def cumsum_matmul(nb):
        later_sum[nb] = jnp.dot(this_or_later, soft_bf.pop(nb), preferred_element_type=F32)

    def weights(nb):
        hd, i, j = blocks[nb]
        inc = later_sum.pop(nb)
        x = logit.pop(nb) - inc
        if j != i:
            x = x - carry[hd, i]
        a = jnp.exp2(x)
        if j == i:
            a = jnp.where(causal, a, 0.0)
        total = inc[0:1, :]
        carry[hd, i] = total if j == i else carry[hd, i] + total
        a_bf[nb] = a.astype(BF16)

    def value_matmul(nb):
        hd, i, j = blocks[nb]
        d = jnp.dot(vt_ref[hd, :, j * tile:(j + 1) * tile], a_bf.pop(nb), preferred_element_type=F32)
        acc[hd, i] = d if j == i else acc[hd, i] + d
        if j == 0:
            o_ref[i * tile:(i + 1) * tile, head_cols(hd)] = acc.pop((hd, i)).T.astype(o_ref.dtype)

    def transpose_values(hd):
        vt_ref[hd] = v_ref[:, head_cols(hd)].astype(F32).T.astype(BF16)

    def cast_side():
        side_bf_ref[...] = side_ref[...].astype(BF16)

    l_soft, l_cum, l_w, l_pv = _SB_LAGS
    per_head = len(blocks) // heads
    events = [(2, 1, cast_side)]
    events += [(hd * per_head + 1, 1, functools.partial(transpose_values, hd)) for hd in range(heads)]
    for nb in range(len(blocks)):
        events += [
            (nb, 0, functools.partial(logit_matmul, nb)),
            (nb + l_soft + l_cum + l_w + l_pv, 2, functools.partial(value_matmul, nb)),
            (nb + l_soft + l_cum, 3, functools.partial(cumsum_matmul, nb)),
            (nb + l_soft + l_cum + l_w, 4, functools.partial(weights, nb)),
            (nb + l_soft, 5, functools.partial(softplus, nb)),
        ]
    _emit_in_order(events)


def _sb_attn(qkv, side, *, batch, seq, tile, heads_per_step):
    kern = functools.partial(_sb_kernel, seq=seq, tile=tile, heads=heads_per_step)
    groups = N_HEADS_SB // heads_per_step
    base = 3 * (N_HEADS_MOBA // heads_per_step)
    hq, hk, hv = base, base + groups, base + 2 * groups
    blk = (seq, heads_per_step * HEAD_DIM)
    side_in, side_out, side_shape = _side_cast_specs(
        side, batch * groups, lambda b, h: (b * groups + h, 0))
    return pl.pallas_call(
        kern,
        out_shape=(jax.ShapeDtypeStruct((batch * seq, N_HEADS_SB * HEAD_DIM), ATTN_OUT_DTYPE), side_shape),
        grid=(batch, groups),
        in_specs=[
            pl.BlockSpec(blk, lambda b, h: (b, hq + h)),
            pl.BlockSpec(blk, lambda b, h: (b, hk + h)),
            pl.BlockSpec(blk, lambda b, h: (b, hv + h)),
            side_in,
        ],
        out_specs=(pl.BlockSpec(blk, lambda b, h: (b, h)), side_out),
        scratch_shapes=[pltpu.VMEM((heads_per_step, HEAD_DIM, seq), BF16)],
        compiler_params=pltpu.CompilerParams(
            dimension_semantics=("arbitrary", "arbitrary"),
            vmem_limit_bytes=VMEM_LIMIT_BYTES),
        name="sb_attn",
    )(qkv, qkv, qkv, side)


def _out_proj_kernel(oa_ref, ob_ref, ga_ref, gb_ref, w_ref, x_ref, y_ref):
    d_a = oa_ref.shape[1]
    oa = oa_ref[...].astype(F32)
    ob = ob_ref[...].astype(F32)
    na = ((oa * _rms_scale(oa)) * ga_ref[...]).astype(BF16)
    nb = ((ob * _rms_scale(ob)) * gb_ref[...]).astype(BF16)
    n_chunk = 4
    cw = y_ref.shape[1] // n_chunk
    for c in range(n_chunk):
        cols = slice(c * cw, (c + 1) * cw)
        y = (jnp.dot(na, w_ref[0:d_a, cols], preferred_element_type=F32)
             + jnp.dot(nb, w_ref[d_a:, cols], preferred_element_type=F32))
        y_ref[:, cols] = x_ref[:, cols] + y


def _out_proj(o_a, o_b, g_a, g_b, w_bf16, x2, *, tm):
    m, d_a = o_a.shape
    d_b = o_b.shape[1]
    d = w_bf16.shape[1]
    return pl.pallas_call(
        _out_proj_kernel,
        out_shape=jax.ShapeDtypeStruct((m, d), F32),
        grid=(m // tm,),
        in_specs=[
            pl.BlockSpec((tm, d_a), lambda i: (i, 0)),
            pl.BlockSpec((tm, d_b), lambda i: (i, 0)),
            pl.BlockSpec((1, d_a), lambda i: (0, 0)),
            pl.BlockSpec((1, d_b), lambda i: (0, 0)),
            pl.BlockSpec((d_a + d_b, d), lambda i: (0, 0), pipeline_mode=pl.Buffered(1)),
            pl.BlockSpec((tm, d), lambda i: (i, 0)),
        ],
        out_specs=pl.BlockSpec((tm, d), lambda i: (i, 0)),
        compiler_params=pltpu.CompilerParams(
            dimension_semantics=("arbitrary",),
            vmem_limit_bytes=LARGE_VMEM_LIMIT_BYTES),
        name="out_proj",
    )(o_a, o_b, g_a, g_b, w_bf16, x2)


def _mlp_kernel(x_ref, g_ref, wu_ref, wd_ref, gf_ref, o_ref, h_ref, *, n_chunk):
    f = pl.program_id(1)
    n_f = pl.num_programs(1)

    cw = o_ref.shape[1] // n_chunk

    def mlp_slice(h, base):
        u = jnp.dot(h, wu_ref[...], preferred_element_type=F32)
        r = jnp.maximum(u, 0.0)
        act = (r * r).astype(BF16)
        for c in range(n_chunk):
            cols = slice(c * cw, (c + 1) * cw)
            o_ref[:, cols] = base(cols) + jnp.dot(act, wd_ref[:, cols], preferred_element_type=F32)

    @pl.when(f == 0)
    def _():
        x = x_ref[...]
        h = ((x * _rms_scale(x)) * g_ref[...]).astype(BF16)
        h_ref[...] = h
        mlp_slice(h, lambda cols: x_ref[:, cols])

    @pl.when(f > 0)
    def _():
        mlp_slice(h_ref[...], lambda cols: o_ref[:, cols])

    @pl.when(f == n_f - 1)
    def _():
        y = o_ref[...]
        o_ref[...] = (y * _rms_scale(y)) * gf_ref[...]


def _mlp(x1, g, wu_bf16, wd_bf16, g_final, *, tm, tf):
    m, d = x1.shape
    d_ff = wu_bf16.shape[1]
    kern = functools.partial(_mlp_kernel, n_chunk=4)
    return pl.pallas_call(
        kern,
        out_shape=jax.ShapeDtypeStruct((m, d), F32),
        grid=(m // tm, d_ff // tf),
        in_specs=[
            pl.BlockSpec((tm, d), lambda i, f: (i, 0)),
            pl.BlockSpec((1, d), lambda i, f: (0, 0)),
            pl.BlockSpec((d, tf), lambda i, f: (0, f)),
            pl.BlockSpec((tf, d), lambda i, f: (f, 0)),
            pl.BlockSpec((1, d), lambda i, f: (0, 0)),
        ],
        out_specs=pl.BlockSpec((tm, d), lambda i, f: (i, 0)),
        scratch_shapes=[pltpu.VMEM((tm, d), BF16)],
        compiler_params=pltpu.CompilerParams(
            dimension_semantics=("arbitrary", "arbitrary"),
            vmem_limit_bytes=LARGE_VMEM_LIMIT_BYTES),
        name="mlp",
    )(x1, g, wu_bf16, wd_bf16, g_final)


def _rope_tables(seq):
    half = ROPE_DIMS // 2
    inv_freq = ROPE_THETA ** (-np.arange(half, dtype=np.float64) / half)
    ang = np.arange(seq, dtype=np.float64)[:, None] * inv_freq[None, :]
    cos, sin = np.cos(ang), np.sin(ang)
    ones = np.ones((seq, HEAD_DIM - ROPE_DIMS))
    zeros_rest = np.zeros((seq, HEAD_DIM - ROPE_DIMS))
    cos_t = np.concatenate([cos, cos, ones], axis=1)
    sin_t = np.concatenate([-sin, sin, zeros_rest], axis=1)
    return jnp.asarray(cos_t, F32), jnp.asarray(sin_t, F32)


def kernel(x, mix_norm_g, w_in, moba_out_g, sb_out_g, w_out, mlp_norm_g, w_up, w_down, final_norm_g):
    batch, seq, d_model = x.shape
    depth = w_in.shape[0]
    cos_t, sin_t = _rope_tables(seq)
    x2 = x.reshape(batch * seq, d_model)
    for l in range(depth):
        qkv = _qkv_proj(x2, mix_norm_g[l][None, :], w_in[l], cos_t, sin_t,
                        seq=seq, tm=1024, tn=1024, n_groups=2)
        o_a, w_up_bf, w_out_bf = _moba_attn(qkv, w_up[l], w_out[l], batch=batch, seq=seq,
                                            heads_per_step=ATTN_HEADS_PER_STEP)
        o_b, w_down_bf = _sb_attn(qkv, w_down[l], batch=batch, seq=seq, tile=256,
                                  heads_per_step=ATTN_HEADS_PER_STEP)
        x1 = _out_proj(o_a, o_b, moba_out_g[l][None, :], sb_out_g[l][None, :],
                       w_out_bf, x2, tm=1024)
        last = l == depth - 1
        assert last, "kernel fuses the final RMSNorm into the last layer's MLP; DEPTH must be 1"
        x2 = _mlp(x1, mlp_norm_g[l][None, :], w_up_bf, w_down_bf,
                  final_norm_g[None, :], tm=1024, tf=1024)
    return x2.reshape(batch, seq, d_model)
```

```python
import functools
import math

import jax
import jax.numpy as jnp
import numpy as np
from jax import lax
from jax.experimental import pallas as pl
from jax.experimental.pallas import tpu as pltpu

HEAD_DIM = 128
N_HEADS_MOBA = 8
N_HEADS_SB = 8
MOBA_BLOCK = 256
MOBA_TOPK = 3
ROPE_THETA = 500000.0
ROPE_DIMS = HEAD_DIM // 4
EPS = 1e-6
NEG = -1e30

F32 = jnp.float32
BF16 = jnp.bfloat16
ATTN_OUT_DTYPE = BF16
ATTN_HEADS_PER_STEP = 4

_NT = (((1,), (1,)), ((), ()))

VMEM_LIMIT_BYTES = 56 * 1024 * 1024
LARGE_VMEM_LIMIT_BYTES = 62 * 1024 * 1024

Q_SCALE = HEAD_DIM ** -0.5 * math.log2(math.e)


def _rms_scale(x):
    return lax.rsqrt(jnp.mean(x * x, axis=-1, keepdims=True) + EPS)


def _emit_in_order(events):
    for _, _, thunk in sorted(events, key=lambda e: (e[0], e[1])):
        thunk()


def _qkv_kernel(x_ref, g_ref, w_ref, cos_ref, sin_ref, o_ref, h_ref):
    j = pl.program_id(1)
    i = pl.program_id(2)
    tm = x_ref.shape[0]
    rows = pl.ds(pl.multiple_of(i * tm, tm), tm)

    def project(h=None):
        h = h_ref[rows, :] if h is None else h
        return jnp.dot(h, w_ref[...].astype(BF16), preferred_element_type=F32)

    def rope(y, post_scale):
        n_heads = y.shape[1] // HEAD_DIM
        half = ROPE_DIMS // 2
        lane = lax.broadcasted_iota(jnp.int32, (y.shape[0], HEAD_DIM), 1)
        partner = jnp.where(lane < ROPE_DIMS, lane ^ half, lane)
        for hd in range(n_heads):
            t = y[:, hd * HEAD_DIM:(hd + 1) * HEAD_DIM]
            swapped = jnp.take_along_axis(t, partner, axis=1)
            r = t * cos_ref[...] + swapped * sin_ref[...]
            if post_scale is not None:
                r = r * post_scale
            o_ref[:, hd * HEAD_DIM:(hd + 1) * HEAD_DIM] = r.astype(o_ref.dtype)

    @pl.when(j == 0)
    def _():
        x = x_ref[...]
        h = ((x * _rms_scale(x)) * g_ref[...]).astype(BF16)
        h_ref[rows, :] = h
        rope(project(h), Q_SCALE)

    @pl.when(j == 1)
    def _():
        rope(project(), None)

    @pl.when(j == 3)
    def _():
        o_ref[...] = (project() * Q_SCALE).astype(o_ref.dtype)

    @pl.when((j == 2) | (j >= 4))
    def _():
        o_ref[...] = project().astype(o_ref.dtype)


def _side_cast_specs(side, n_steps, index_map):
    rows, cols = side.shape
    slab = rows // n_steps
    assert slab * n_steps == rows and slab % 16 == 0, (rows, n_steps)
    spec = pl.BlockSpec((slab, cols), index_map)
    return spec, spec, jax.ShapeDtypeStruct((rows, cols), BF16)


def _qkv_proj(x2, g, w, cos_t, sin_t, *, seq, tm, tn, n_groups):
    m, d = x2.shape
    n = w.shape[1]
    tiles = m // (tm * n_groups)
    pos_blocks = seq // tm
    n_rope_tiles = 2

    def x_tile(gr, j, i):
        return gr * tiles + jnp.where(j == 0, i, tiles - 1)

    tab_spec = pl.BlockSpec(
        (tm, HEAD_DIM), lambda gr, j, i: (jnp.where(j < n_rope_tiles, (gr * tiles + i) % pos_blocks, 0), 0))
    return pl.pallas_call(
        _qkv_kernel,
        out_shape=jax.ShapeDtypeStruct((m, n), BF16),
        grid=(n_groups, n // tn, tiles),
        in_specs=[
            pl.BlockSpec((tm, d), lambda gr, j, i: (x_tile(gr, j, i), 0)),
            pl.BlockSpec((1, d), lambda gr, j, i: (0, 0)),
            pl.BlockSpec((d, tn), lambda gr, j, i: (0, j)),
            tab_spec, tab_spec,
        ],
        out_specs=pl.BlockSpec((tm, tn), lambda gr, j, i: (gr * tiles + i, j)),
        scratch_shapes=[pltpu.VMEM((tiles * tm, d), BF16)],
        compiler_params=pltpu.CompilerParams(
            dimension_semantics=("arbitrary", "arbitrary", "arbitrary"),
            vmem_limit_bytes=LARGE_VMEM_LIMIT_BYTES),
        name="qkv_proj",
    )(x2, g, w, cos_t, sin_t)


_MOBA_LAG_MASK, _MOBA_LAG_EXP, _MOBA_LAG_PV = 3, 3, 2
_MOBA_SUM_ROWS = 16


def _moba_kernel(q_ref, k_ref, v_ref, side_ref, side2_ref, o_ref, side_bf_ref, side2_bf_ref, vt_ref,
                 *, seq, heads):
    blk = MOBA_BLOCK
    n_blk = seq // blk

    key_i = lax.broadcasted_iota(jnp.int32, (blk, blk), 0)
    qry_i = lax.broadcasted_iota(jnp.int32, (blk, blk), 1)
    causal = key_i <= qry_i
    blk_id = lax.broadcasted_iota(jnp.int32, (n_blk, blk), 0)

    def head_cols(hd):
        return slice(hd * HEAD_DIM, (hd + 1) * HEAD_DIM)

    blocks = [(hd, i, j) for hd in range(heads)
              for i in range(n_blk - 1, -1, -1) for j in ([i] + list(range(i)))]
    last_of_tile = {i: (i - 1 if i else 0) for i in range(n_blk)}
    km_rows, raw, gate_raw, bias, scores, col_max, p_bf, acc = {}, {}, {}, {}, {}, {}, {}, {}

    def mean_keys(hd):
        kf = k_ref[:, head_cols(hd)].astype(F32)
        k_mean = jnp.concatenate(
            [jnp.mean(kf[n * blk:(n + 1) * blk, :], axis=0, keepdims=True) for n in range(n_blk)], axis=0)
        km_hi = k_mean.astype(BF16)
        km_lo = (k_mean - km_hi.astype(F32)).astype(BF16)
        km_rows[hd] = jnp.concatenate([km_hi, km_lo], axis=0)

    def transpose_values(hd):
        vt_ref[hd, 0:HEAD_DIM, :] = v_ref[:, head_cols(hd)].astype(F32).T.astype(BF16)
        extra = lax.broadcasted_iota(jnp.int32, (_MOBA_SUM_ROWS, seq), 0)
        vt_ref[hd, HEAD_DIM:, :] = (extra == 0).astype(BF16)

    def score_matmul(nb):
        hd, i, j = blocks[nb]
        q_i = q_ref[i * blk:(i + 1) * blk, head_cols(hd)]
        k_j = k_ref[j * blk:(j + 1) * blk, head_cols(hd)]
        if j == i:
            r = lax.dot_general(jnp.concatenate([k_j, km_rows[hd]], axis=0), q_i, _NT,
                                preferred_element_type=F32)
            raw[nb] = r[0:blk, :]
            gate_raw[hd, i] = r[blk:blk + n_blk, :] + r[blk + n_blk:blk + 2 * n_blk, :]
        else:
            raw[nb] = lax.dot_general(k_j, q_i, _NT, preferred_element_type=F32)

    def select_blocks(hd, i):
        past = blk_id < i
        g = jnp.where(past, gate_raw.pop((hd, i)), NEG)
        rank = jnp.zeros((n_blk, blk), jnp.int32)
        for other in range(n_blk):
            g_o = g[other:other + 1, :]
            beats = (g_o > g) | ((g_o == g) & (other < blk_id))
            rank = rank + beats.astype(jnp.int32)
        bias[hd, i] = jnp.where(past & (rank < MOBA_TOPK), 0.0, NEG).astype(F32)

    def mask_and_max(nb):
        hd, i, j = blocks[nb]
        s = raw.pop(nb)
        if j == i:
            s = jnp.where(causal, s, NEG)
            select_blocks(hd, i)
        else:
            s = s + bias[hd, i][j:j + 1, :]
        scores[nb] = s
        cm = jnp.max(s, axis=0, keepdims=True)
        col_max[hd, i] = cm if j == i else jnp.maximum(col_max[hd, i], cm)

    def exponentiate(nb):
        hd, i, j = blocks[nb]
        p = jnp.exp2(scores.pop(nb) - col_max[hd, i])
        p_bf[nb] = p.astype(BF16)

    def value_matmul(nb):
        hd, i, j = blocks[nb]
        d = jnp.dot(vt_ref[hd, :, j * blk:(j + 1) * blk], p_bf.pop(nb), preferred_element_type=F32)
        acc[hd, i] = d if j == i else acc[hd, i] + d
        if j == last_of_tile[i]:
            total = acc.pop((hd, i))
            out = (total[0:HEAD_DIM, :] / total[HEAD_DIM:HEAD_DIM + 1, :]).T
            o_ref[i * blk:(i + 1) * blk, head_cols(hd)] = out.astype(o_ref.dtype)

    def cast_side():
        side_bf_ref[...] = side_ref[...].astype(BF16)
        side2_bf_ref[...] = side2_ref[...].astype(BF16)

    mask_step = [nb + _MOBA_LAG_MASK for nb in range(len(blocks))]
    tile_done = {}
    for nb, (hd, i, _) in enumerate(blocks):
        tile_done[hd, i] = max(tile_done.get((hd, i), 0), mask_step[nb])
    per_head = len(blocks) // heads
    events = [(2, 1, cast_side)]
    for hd in range(heads):
        first = hd * per_head
        events += [(max(first - _MOBA_LAG_MASK, 0), -1, functools.partial(mean_keys, hd)),
                   (first + 1, 1, functools.partial(transpose_values, hd))]
    exp_step = -1
    for nb, (hd, i, _) in enumerate(blocks):
        exp_step = max(exp_step + 1, tile_done[hd, i] + _MOBA_LAG_EXP)
        events += [
            (nb, 0, functools.partial(score_matmul, nb)),
            (exp_step + _MOBA_LAG_PV, 2, functools.partial(value_matmul, nb)),
            (exp_step, 3, functools.partial(exponentiate, nb)),
            (mask_step[nb], 4, functools.partial(mask_and_max, nb)),
        ]
    _emit_in_order(events)


def _moba_attn(qkv, side, side2, *, batch, seq, heads_per_step):
    kern = functools.partial(_moba_kernel, seq=seq, heads=heads_per_step)
    groups = N_HEADS_MOBA // heads_per_step
    hq, hk, hv = 0, groups, 2 * groups
    blk = (seq, heads_per_step * HEAD_DIM)
    slab_of_step = lambda b, h: (b * groups + h, 0)
    side_in, side_out, side_shape = _side_cast_specs(side, batch * groups, slab_of_step)
    side2_in, side2_out, side2_shape = _side_cast_specs(side2, batch * groups, slab_of_step)
    return pl.pallas_call(
        kern,
        out_shape=(jax.ShapeDtypeStruct((batch * seq, N_HEADS_MOBA * HEAD_DIM), ATTN_OUT_DTYPE),
                   side_shape, side2_shape),
        grid=(batch, groups),
        in_specs=[
            pl.BlockSpec(blk, lambda b, h: (b, hq + h)),
            pl.BlockSpec(blk, lambda b, h: (b, hk + h)),
            pl.BlockSpec(blk, lambda b, h: (b, hv + h)),
            side_in, side2_in,
        ],
        out_specs=(pl.BlockSpec(blk, lambda b, h: (b, h)), side_out, side2_out),
        scratch_shapes=[pltpu.VMEM((heads_per_step, HEAD_DIM + _MOBA_SUM_ROWS, seq), BF16)],
        compiler_params=pltpu.CompilerParams(
            dimension_semantics=("arbitrary", "arbitrary"),
            vmem_limit_bytes=VMEM_LIMIT_BYTES),
        name="moba_attn",
    )(qkv, qkv, qkv, side, side2)


_SB_LAGS = (1, 1, 2, 1)
_LOG2E_HI = float(np.asarray(math.log2(math.e), dtype=jnp.bfloat16))
_LOG2E_LO = float(np.asarray(math.log2(math.e) - _LOG2E_HI, dtype=jnp.bfloat16))


def _sb_kernel(q_ref, k_ref, v_ref, side_ref, o_ref, side_bf_ref, vt_ref, *, seq, tile, heads):
    n_tiles = seq // tile
    key_i = lax.broadcasted_iota(jnp.int32, (tile, tile), 0)
    qry_i = lax.broadcasted_iota(jnp.int32, (tile, tile), 1)
    causal = key_i < qry_i
    this_or_later = (qry_i >= key_i).astype(BF16)

    def head_cols(hd):
        return slice(hd * HEAD_DIM, (hd + 1) * HEAD_DIM)

    blocks = [(hd, i, j) for hd in range(heads)
              for i in range(n_tiles - 1, -1, -1) for j in range(i, -1, -1)]
    raw, logit, soft_bf, later_sum, a_bf, acc, carry = {}, {}, {}, {}, {}, {}, {}

    def logit_matmul(nb):
        hd, i, j = blocks[nb]
        raw[nb] = lax.dot_general(k_ref[j * tile:(j + 1) * tile, head_cols(hd)],
                                  q_ref[i * tile:(i + 1) * tile, head_cols(hd)],
                                  _NT, preferred_element_type=F32)

    def softplus(nb):
        _, i, j = blocks[nb]
        z = raw.pop(nb)
        zb = z.astype(BF16)
        ln_term = jnp.log(1.0 + jnp.exp2(-jnp.abs(zb)))
        t = jnp.maximum(zb, 0.0) + (ln_term * _LOG2E_HI + ln_term * _LOG2E_LO)
        if j == i:
            t = jnp.where(causal, t, 0.0)
        logit[nb] = z
        soft_bf[nb] = t.astype(BF16)

    def cumsum_matmul(nb):
        later_sum[nb] = jnp.dot(this_or_later, soft_bf.pop(nb), preferred_element_type=F32)

    def weights(nb):
        hd, i, j = blocks[nb]
        inc = later_sum.pop(nb)
        x = logit.pop(nb) - inc
        if j != i:
            x = x - carry[hd, i]
        a = jnp.exp2(x)
        if j == i:
            a = jnp.where(causal, a, 0.0)
        total = inc[0:1, :]
        carry[hd, i] = total if j == i else carry[hd, i] + total
        a_bf[nb] = a.astype(BF16)

    def value_matmul(nb):
        hd, i, j = blocks[nb]
        d = jnp.dot(vt_ref[hd, :, j * tile:(j + 1) * tile], a_bf.pop(nb), preferred_element_type=F32)
        acc[hd, i] = d if j == i else acc[hd, i] + d
        if j == 0:
            o_ref[i * tile:(i + 1) * tile, head_cols(hd)] = acc.pop((hd, i)).T.astype(o_ref.dtype)

    def transpose_values(hd):
        vt_ref[hd] = v_ref[:, head_cols(hd)].astype(F32).T.astype(BF16)

    def cast_side():
        side_bf_ref[...] = side_ref[...].astype(BF16)

    l_soft, l_cum, l_w, l_pv = _SB_LAGS
    per_head = len(blocks) // heads
    events = [(2, 1, cast_side)]
    events += [(hd * per_head + 1, 1, functools.partial(transpose_values, hd)) for hd in range(heads)]
    for nb in range(len(blocks)):
        events += [
            (nb, 0, functools.partial(logit_matmul, nb)),
            (nb + l_soft + l_cum + l_w + l_pv, 2, functools.partial(value_matmul, nb)),
            (nb + l_soft + l_cum, 3, functools.partial(cumsum_matmul, nb)),
            (nb + l_soft + l_cum + l_w, 4, functools.partial(weights, nb)),
            (nb + l_soft, 5, functools.partial(softplus, nb)),
        ]
    _emit_in_order(events)


def _sb_attn(qkv, side, *, batch, seq, tile, heads_per_step):
    kern = functools.partial(_sb_kernel, seq=seq, tile=tile, heads=heads_per_step)
    groups = N_HEADS_SB // heads_per_step
    base = 3 * (N_HEADS_MOBA // heads_per_step)
    hq, hk, hv = base, base + groups, base + 2 * groups
    blk = (seq, heads_per_step * HEAD_DIM)
    side_in, side_out, side_shape = _side_cast_specs(
        side, batch * groups, lambda b, h: (b * groups + h, 0))
    return pl.pallas_call(
        kern,
        out_shape=(jax.ShapeDtypeStruct((batch * seq, N_HEADS_SB * HEAD_DIM), ATTN_OUT_DTYPE), side_shape),
        grid=(batch, groups),
        in_specs=[
            pl.BlockSpec(blk, lambda b, h: (b, hq + h)),
            pl.BlockSpec(blk, lambda b, h: (b, hk + h)),
            pl.BlockSpec(blk, lambda b, h: (b, hv + h)),
            side_in,
        ],
        out_specs=(pl.BlockSpec(blk, lambda b, h: (b, h)), side_out),
        scratch_shapes=[pltpu.VMEM((heads_per_step, HEAD_DIM, seq), BF16)],
        compiler_params=pltpu.CompilerParams(
            dimension_semantics=("arbitrary", "arbitrary"),
            vmem_limit_bytes=VMEM_LIMIT_BYTES),
        name="sb_attn",
    )(qkv, qkv, qkv, side)


def _out_proj_kernel(oa_ref, ob_ref, ga_ref, gb_ref, w_ref, x_ref, y_ref):
    d_a = oa_ref.shape[1]
    oa = oa_ref[...].astype(F32)
    ob = ob_ref[...].astype(F32)
    na = ((oa * _rms_scale(oa)) * ga_ref[...]).astype(BF16)
    nb = ((ob * _rms_scale(ob)) * gb_ref[...]).astype(BF16)
    n_chunk = 4
    cw = y_ref.shape[1] // n_chunk
    for c in range(n_chunk):
        cols = slice(c * cw, (c + 1) * cw)
        y = (jnp.dot(na, w_ref[0:d_a, cols], preferred_element_type=F32)
             + jnp.dot(nb, w_ref[d_a:, cols], preferred_element_type=F32))
        y_ref[:, cols] = x_ref[:, cols] + y


def _out_proj(o_a, o_b, g_a, g_b, w_bf16, x2, *, tm):
    m, d_a = o_a.shape
    d_b = o_b.shape[1]
    d = w_bf16.shape[1]
    return pl.pallas_call(
        _out_proj_kernel,
        out_shape=jax.ShapeDtypeStruct((m, d), F32),
        grid=(m // tm,),
        in_specs=[
            pl.BlockSpec((tm, d_a), lambda i: (i, 0)),
            pl.BlockSpec((tm, d_b), lambda i: (i, 0)),
            pl.BlockSpec((1, d_a), lambda i: (0, 0)),
            pl.BlockSpec((1, d_b), lambda i: (0, 0)),
            pl.BlockSpec((d_a + d_b, d), lambda i: (0, 0), pipeline_mode=pl.Buffered(1)),
            pl.BlockSpec((tm, d), lambda i: (i, 0)),
        ],
        out_specs=pl.BlockSpec((tm, d), lambda i: (i, 0)),
        compiler_params=pltpu.CompilerParams(
            dimension_semantics=("arbitrary",),
            vmem_limit_bytes=LARGE_VMEM_LIMIT_BYTES),
        name="out_proj",
    )(o_a, o_b, g_a, g_b, w_bf16, x2)


def _mlp_kernel(x_ref, g_ref, wu_ref, wd_ref, gf_ref, o_ref, h_ref, *, n_chunk):
    f = pl.program_id(1)
    n_f = pl.num_programs(1)

    cw = o_ref.shape[1] // n_chunk

    def mlp_slice(h, base):
        u = jnp.dot(h, wu_ref[...], preferred_element_type=F32)
        r = jnp.maximum(u, 0.0)
        act = (r * r).astype(BF16)
        for c in range(n_chunk):
            cols = slice(c * cw, (c + 1) * cw)
            o_ref[:, cols] = base(cols) + jnp.dot(act, wd_ref[:, cols], preferred_element_type=F32)

    @pl.when(f == 0)
    def _():
        x = x_ref[...]
        h = ((x * _rms_scale(x)) * g_ref[...]).astype(BF16)
        h_ref[...] = h
        mlp_slice(h, lambda cols: x_ref[:, cols])

    @pl.when(f > 0)
    def _():
        mlp_slice(h_ref[...], lambda cols: o_ref[:, cols])

    @pl.when(f == n_f - 1)
    def _():
        y = o_ref[...]
        o_ref[...] = (y * _rms_scale(y)) * gf_ref[...]


def _mlp(x1, g, wu_bf16, wd_bf16, g_final, *, tm, tf):
    m, d = x1.shape
    d_ff = wu_bf16.shape[1]
    kern = functools.partial(_mlp_kernel, n_chunk=4)
    return pl.pallas_call(
        kern,
        out_shape=jax.ShapeDtypeStruct((m, d), F32),
        grid=(m // tm, d_ff // tf),
        in_specs=[
            pl.BlockSpec((tm, d), lambda i, f: (i, 0)),
            pl.BlockSpec((1, d), lambda i, f: (0, 0)),
            pl.BlockSpec((d, tf), lambda i, f: (0, f)),
            pl.BlockSpec((tf, d), lambda i, f: (f, 0)),
            pl.BlockSpec((1, d), lambda i, f: (0, 0)),
        ],
        out_specs=pl.BlockSpec((tm, d), lambda i, f: (i, 0)),
        scratch_shapes=[pltpu.VMEM((tm, d), BF16)],
        compiler_params=pltpu.CompilerParams(
            dimension_semantics=("arbitrary", "arbitrary"),
            vmem_limit_bytes=LARGE_VMEM_LIMIT_BYTES),
        name="mlp",
    )(x1, g, wu_bf16, wd_bf16, g_final)


def _rope_tables(seq):
    half = ROPE_DIMS // 2
    inv_freq = ROPE_THETA ** (-np.arange(half, dtype=np.float64) / half)
    ang = np.arange(seq, dtype=np.float64)[:, None] * inv_freq[None, :]
    cos, sin = np.cos(ang), np.sin(ang)
    ones = np.ones((seq, HEAD_DIM - ROPE_DIMS))
    zeros_rest = np.zeros((seq, HEAD_DIM - ROPE_DIMS))
    cos_t = np.concatenate([cos, cos, ones], axis=1)
    sin_t = np.concatenate([-sin, sin, zeros_rest], axis=1)
    return jnp.asarray(cos_t, F32), jnp.asarray(sin_t, F32)


def kernel(x, mix_norm_g, w_in, moba_out_g, sb_out_g, w_out, mlp_norm_g, w_up, w_down, final_norm_g):
    batch, seq, d_model = x.shape
    depth = w_in.shape[0]
    cos_t, sin_t = _rope_tables(seq)
    x2 = x.reshape(batch * seq, d_model)
    for l in range(depth):
        qkv = _qkv_proj(x2, mix_norm_g[l][None, :], w_in[l], cos_t, sin_t,
                        seq=seq, tm=1024, tn=1024, n_groups=2)
        o_a, w_up_bf, w_out_bf = _moba_attn(qkv, w_up[l], w_out[l], batch=batch, seq=seq,
                                            heads_per_step=ATTN_HEADS_PER_STEP)
        o_b, w_down_bf = _sb_attn(qkv, w_down[l], batch=batch, seq=seq, tile=256,
                                  heads_per_step=ATTN_HEADS_PER_STEP)
        x1 = _out_proj(o_a, o_b, moba_out_g[l][None, :], sb_out_g[l][None, :],
                       w_out_bf, x2, tm=1024)
        last = l == depth - 1
        assert last, "kernel fuses the final RMSNorm into the last layer's MLP; DEPTH must be 1"
        x2 = _mlp(x1, mlp_norm_g[l][None, :], w_up_bf, w_down_bf,
                  final_norm_g[None, :], tm=1024, tf=1024)
    return x2.reshape(batch, seq, d_model)
```

```python
import functools
import math

import jax
import jax.numpy as jnp
import numpy as np
from jax import lax
from jax.experimental import pallas as pl
from jax.experimental.pallas import tpu as pltpu

HEAD_DIM = 128
N_HEADS_MOBA = 8
N_HEADS_SB = 8
MOBA_BLOCK = 256
MOBA_TOPK = 3
ROPE_THETA = 500000.0
ROPE_DIMS = HEAD_DIM // 4
EPS = 1e-6
NEG = -1e30

F32 = jnp.float32
BF16 = jnp.bfloat16
ATTN_OUT_DTYPE = BF16
ATTN_HEADS_PER_STEP = 2

_NT = (((1,), (1,)), ((), ()))

VMEM_LIMIT_BYTES = 56 * 1024 * 1024
LARGE_VMEM_LIMIT_BYTES = 62 * 1024 * 1024

QKV_ROW_TILE, QKV_COL_TILE, QKV_ROW_GROUPS = 1024, 1024, 2
OUT_PROJ_ROW_TILE = 1024
MLP_ROW_TILE, MLP_FF_TILE = 1024, 1024
SB_TILE = 256
EPILOGUE_CHUNKS = 4

Q_SCALE = HEAD_DIM ** -0.5 * math.log2(math.e)


def _rms_scale(x):
    return lax.rsqrt(jnp.mean(x * x, axis=-1, keepdims=True) + EPS)


def _emit_in_order(events):
    for _, _, thunk in sorted(events, key=lambda e: (e[0], e[1])):
        thunk()


def _qkv_kernel(x_ref, g_ref, w_ref, cos_ref, sin_ref, o_ref, h_ref):
    j = pl.program_id(1)
    i = pl.program_id(2)
    tm = x_ref.shape[0]
    rows = pl.ds(pl.multiple_of(i * tm, tm), tm)

    def project(h=None):
        h = h_ref[rows, :] if h is None else h
        return jnp.dot(h, w_ref[...].astype(BF16), preferred_element_type=F32)

    def rope(y, post_scale):
        n_heads = y.shape[1] // HEAD_DIM
        half = ROPE_DIMS // 2
        lane = lax.broadcasted_iota(jnp.int32, (y.shape[0], HEAD_DIM), 1)
        partner = jnp.where(lane < ROPE_DIMS, lane ^ half, lane)
        for hd in range(n_heads):
            t = y[:, hd * HEAD_DIM:(hd + 1) * HEAD_DIM]
            swapped = jnp.take_along_axis(t, partner, axis=1)
            r = t * cos_ref[...] + swapped * sin_ref[...]
            if post_scale is not None:
                r = r * post_scale
            o_ref[:, hd * HEAD_DIM:(hd + 1) * HEAD_DIM] = r.astype(o_ref.dtype)

    @pl.when(j == 0)
    def _():
        x = x_ref[...]
        h = ((x * _rms_scale(x)) * g_ref[...]).astype(BF16)
        h_ref[rows, :] = h
        rope(project(h), Q_SCALE)

    @pl.when(j == 1)
    def _():
        rope(project(), None)

    @pl.when(j == 3)
    def _():
        o_ref[...] = (project() * Q_SCALE).astype(o_ref.dtype)

    @pl.when((j == 2) | (j >= 4))
    def _():
        o_ref[...] = project().astype(o_ref.dtype)


def _side_cast_specs(side, n_steps, index_map):
    rows, cols = side.shape
    slab = rows // n_steps
    assert slab * n_steps == rows and slab % 16 == 0, (rows, n_steps)
    spec = pl.BlockSpec((slab, cols), index_map)
    return spec, spec, jax.ShapeDtypeStruct((rows, cols), BF16)


def _qkv_proj(x2, g, w, cos_t, sin_t, *, seq, tm, tn, n_groups):
    m, d = x2.shape
    n = w.shape[1]
    assert tn == N_HEADS_MOBA * HEAD_DIM == N_HEADS_SB * HEAD_DIM and n == 6 * tn, (tn, n)
    tiles = m // (tm * n_groups)
    pos_blocks = seq // tm
    n_rope_tiles = 2

    def x_tile(gr, j, i):
        return gr * tiles + jnp.where(j == 0, i, tiles - 1)

    tab_spec = pl.BlockSpec(
        (tm, HEAD_DIM), lambda gr, j, i: (jnp.where(j < n_rope_tiles, (gr * tiles + i) % pos_blocks, 0), 0))
    return pl.pallas_call(
        _qkv_kernel,
        out_shape=jax.ShapeDtypeStruct((m, n), BF16),
        grid=(n_groups, n // tn, tiles),
        in_specs=[
            pl.BlockSpec((tm, d), lambda gr, j, i: (x_tile(gr, j, i), 0)),
            pl.BlockSpec((1, d), lambda gr, j, i: (0, 0)),
            pl.BlockSpec((d, tn), lambda gr, j, i: (0, j)),
            tab_spec, tab_spec,
        ],
        out_specs=pl.BlockSpec((tm, tn), lambda gr, j, i: (gr * tiles + i, j)),
        scratch_shapes=[pltpu.VMEM((tiles * tm, d), BF16)],
        compiler_params=pltpu.CompilerParams(
            dimension_semantics=("arbitrary", "arbitrary", "arbitrary"),
            vmem_limit_bytes=LARGE_VMEM_LIMIT_BYTES),
        name="qkv_proj",
    )(x2, g, w, cos_t, sin_t)


_MOBA_LAG_MASK, _MOBA_LAG_EXP, _MOBA_LAG_PV = 3, 3, 2
_MOBA_SUM_ROWS = 16


def _moba_kernel(q_ref, k_ref, v_ref, side_ref, side2_ref, o_ref, side_bf_ref, side2_bf_ref, vt_ref,
                 *, seq, heads):
    blk = MOBA_BLOCK
    n_blk = seq // blk

    key_i = lax.broadcasted_iota(jnp.int32, (blk, blk), 0)
    qry_i = lax.broadcasted_iota(jnp.int32, (blk, blk), 1)
    causal = key_i <= qry_i
    blk_id = lax.broadcasted_iota(jnp.int32, (n_blk, blk), 0)

    def head_cols(hd):
        return slice(hd * HEAD_DIM, (hd + 1) * HEAD_DIM)

    blocks = [(hd, i, j) for hd in range(heads)
              for i in range(n_blk - 1, -1, -1) for j in ([i] + list(range(i)))]
    last_of_tile = {i: (i - 1 if i else 0) for i in range(n_blk)}
    km_rows, raw, gate_raw, bias, scores, col_max, p_bf, acc = {}, {}, {}, {}, {}, {}, {}, {}

    def mean_keys(hd):
        kf = k_ref[:, head_cols(hd)].astype(F32)
        k_mean = jnp.concatenate(
            [jnp.mean(kf[n * blk:(n + 1) * blk, :], axis=0, keepdims=True) for n in range(n_blk)], axis=0)
        km_hi = k_mean.astype(BF16)
        km_lo = (k_mean - km_hi.astype(F32)).astype(BF16)
        km_rows[hd] = jnp.concatenate([km_hi, km_lo], axis=0)

    def transpose_values(hd):
        vt_ref[hd, 0:HEAD_DIM, :] = v_ref[:, head_cols(hd)].astype(F32).T.astype(BF16)
        extra = lax.broadcasted_iota(jnp.int32, (_MOBA_SUM_ROWS, seq), 0)
        vt_ref[hd, HEAD_DIM:, :] = (extra == 0).astype(BF16)

    def score_matmul(nb):
        hd, i, j = blocks[nb]
        q_i = q_ref[i * blk:(i + 1) * blk, head_cols(hd)]
        k_j = k_ref[j * blk:(j + 1) * blk, head_cols(hd)]
        if j == i:
            r = lax.dot_general(jnp.concatenate([k_j, km_rows[hd]], axis=0), q_i, _NT,
                                preferred_element_type=F32)
            raw[nb] = r[0:blk, :]
            gate_raw[hd, i] = r[blk:blk + n_blk, :] + r[blk + n_blk:blk + 2 * n_blk, :]
        else:
            raw[nb] = lax.dot_general(k_j, q_i, _NT, preferred_element_type=F32)

    def select_blocks(hd, i):
        past = blk_id < i
        g = jnp.where(past, gate_raw.pop((hd, i)), NEG)
        rank = jnp.zeros((n_blk, blk), jnp.int32)
        for other in range(n_blk):
            g_o = g[other:other + 1, :]
            beats = (g_o > g) | ((g_o == g) & (other < blk_id))
            rank = rank + beats.astype(jnp.int32)
        bias[hd, i] = jnp.where(past & (rank < MOBA_TOPK), 0.0, NEG).astype(F32)

    def mask_and_max(nb):
        hd, i, j = blocks[nb]
        s = raw.pop(nb)
        if j == i:
            s = jnp.where(causal, s, NEG)
            select_blocks(hd, i)
        else:
            s = s + bias[hd, i][j:j + 1, :]
        scores[nb] = s
        cm = jnp.max(s, axis=0, keepdims=True)
        col_max[hd, i] = cm if j == i else jnp.maximum(col_max[hd, i], cm)

    def exponentiate(nb):
        hd, i, j = blocks[nb]
        p = jnp.exp2(scores.pop(nb) - col_max[hd, i])
        p_bf[nb] = p.astype(BF16)

    def value_matmul(nb):
        hd, i, j = blocks[nb]
        d = jnp.dot(vt_ref[hd, :, j * blk:(j + 1) * blk], p_bf.pop(nb), preferred_element_type=F32)
        acc[hd, i] = d if j == i else acc[hd, i] + d
        if j == last_of_tile[i]:
            total = acc.pop((hd, i))
            out = (total[0:HEAD_DIM, :] / total[HEAD_DIM:HEAD_DIM + 1, :]).T
            o_ref[i * blk:(i + 1) * blk, head_cols(hd)] = out.astype(o_ref.dtype)

    def cast_side():
        side_bf_ref[...] = side_ref[...].astype(BF16)
        side2_bf_ref[...] = side2_ref[...].astype(BF16)

    mask_step = [nb + _MOBA_LAG_MASK for nb in range(len(blocks))]
    tile_done = {}
    for nb, (hd, i, _) in enumerate(blocks):
        tile_done[hd, i] = max(tile_done.get((hd, i), 0), mask_step[nb])
    per_head = len(blocks) // heads
    events = [(2, 1, cast_side)]
    for hd in range(heads):
        first = hd * per_head
        events += [(max(first - _MOBA_LAG_MASK, 0), -1, functools.partial(mean_keys, hd)),
                   (first + 1, 1, functools.partial(transpose_values, hd))]
    exp_step = -1
    for nb, (hd, i, _) in enumerate(blocks):
        exp_step = max(exp_step + 1, tile_done[hd, i] + _MOBA_LAG_EXP)
        events += [
            (nb, 0, functools.partial(score_matmul, nb)),
            (exp_step + _MOBA_LAG_PV, 2, functools.partial(value_matmul, nb)),
            (exp_step, 3, functools.partial(exponentiate, nb)),
            (mask_step[nb], 4, functools.partial(mask_and_max, nb)),
        ]
    _emit_in_order(events)


def _moba_attn(qkv, side, side2, *, batch, seq, heads_per_step):
    kern = functools.partial(_moba_kernel, seq=seq, heads=heads_per_step)
    groups = N_HEADS_MOBA // heads_per_step
    hq, hk, hv = 0, groups, 2 * groups
    blk = (seq, heads_per_step * HEAD_DIM)
    slab_of_step = lambda b, h: (b * groups + h, 0)
    side_in, side_out, side_shape = _side_cast_specs(side, batch * groups, slab_of_step)
    side2_in, side2_out, side2_shape = _side_cast_specs(side2, batch * groups, slab_of_step)
    return pl.pallas_call(
        kern,
        out_shape=(jax.ShapeDtypeStruct((batch * seq, N_HEADS_MOBA * HEAD_DIM), ATTN_OUT_DTYPE),
                   side_shape, side2_shape),
        grid=(batch, groups),
        in_specs=[
            pl.BlockSpec(blk, lambda b, h: (b, hq + h)),
            pl.BlockSpec(blk, lambda b, h: (b, hk + h)),
            pl.BlockSpec(blk, lambda b, h: (b, hv + h)),
            side_in, side2_in,
        ],
        out_specs=(pl.BlockSpec(blk, lambda b, h: (b, h)), side_out, side2_out),
        scratch_shapes=[pltpu.VMEM((heads_per_step, HEAD_DIM + _MOBA_SUM_ROWS, seq), BF16)],
        compiler_params=pltpu.CompilerParams(
            dimension_semantics=("arbitrary", "arbitrary"),
            vmem_limit_bytes=VMEM_LIMIT_BYTES),
        name="moba_attn",
    )(qkv, qkv, qkv, side, side2)


_SB_LAGS = (1, 1, 2, 1)
_LOG2E_HI = float(np.asarray(math.log2(math.e), dtype=jnp.bfloat16))
_LOG2E_LO = float(np.asarray(math.log2(math.e) - _LOG2E_HI, dtype=jnp.bfloat16))


def _sb_kernel(q_ref, k_ref, v_ref, side_ref, o_ref, side_bf_ref, vt_ref, *, seq, tile, heads):
    n_tiles = seq // tile
    key_i = lax.broadcasted_iota(jnp.int32, (tile, tile), 0)
    qry_i = lax.broadcasted_iota(jnp.int32, (tile, tile), 1)
    causal = key_i < qry_i
    this_or_later = (qry_i >= key_i).astype(BF16)

    def head_cols(hd):
        return slice(hd * HEAD_DIM, (hd + 1) * HEAD_DIM)

    blocks = [(hd, i, j) for hd in range(heads)
              for i in range(n_tiles - 1, -1, -1) for j in range(i, -1, -1)]
    raw, logit, soft_bf, later_sum, a_bf, acc, carry = {}, {}, {}, {}, {}, {}, {}

    def logit_matmul(nb):
        hd, i, j = blocks[nb]
        raw[nb] = lax.dot_general(k_ref[j * tile:(j + 1) * tile, head_cols(hd)],
                                  q_ref[i * tile:(i + 1) * tile, head_cols(hd)],
                                  _NT, preferred_element_type=F32)

    def softplus(nb):
        _, i, j = blocks[nb]
        z = raw.pop(nb)
        zb = z.astype(BF16)
        ln_term = jnp.log(1.0 + jnp.exp2(-jnp.abs(zb)))
        t = jnp.maximum(zb, 0.0) + (ln_term * _LOG2E_HI + ln_term * _LOG2E_LO)
        if j == i:
            t = jnp.where(causal, t, 0.0)
        logit[nb] = z
        soft_bf[nb] = t.astype(BF16)

    def cumsum_matmul(nb):
        later_sum[nb] = jnp.dot(this_or_later, soft_bf.pop(nb), preferred_element_type=F32)

    def weights(nb):
        hd, i, j = blocks[nb]
        inc = later_sum.pop(nb)
        x = logit.pop(nb) - inc
        if j != i:
            x = x - carry[hd, i]
        a = jnp.exp2(x)
        if j == i:
            a = jnp.where(causal, a, 0.0)
        total = inc[0:1, :]
        carry[hd, i] = total if j == i else carry[hd, i] + total
        a_bf[nb] = a.astype(BF16)

    def value_matmul(nb):
        hd, i, j = blocks[nb]
        d = jnp.dot(vt_ref[hd, :, j * tile:(j + 1) * tile], a_bf.pop(nb), preferred_element_type=F32)
        acc[hd, i] = d if j == i else acc[hd, i] + d
        if j == 0:
            o_ref[i * tile:(i + 1) * tile, head_cols(hd)] = acc.pop((hd, i)).T.astype(o_ref.dtype)

    def transpose_values(hd):
        vt_ref[hd] = v_ref[:, head_cols(hd)].astype(F32).T.astype(BF16)

    def cast_side():
        side_bf_ref[...] = side_ref[...].astype(BF16)

    l_soft, l_cum, l_w, l_pv = _SB_LAGS
    per_head = len(blocks) // heads
    events = [(2, 1, cast_side)]
    events += [(hd * per_head + 1, 1, functools.partial(transpose_values, hd)) for hd in range(heads)]
    for nb in range(len(blocks)):
        events += [
            (nb, 0, functools.partial(logit_matmul, nb)),
            (nb + l_soft + l_cum + l_w + l_pv, 2, functools.partial(value_matmul, nb)),
            (nb + l_soft + l_cum, 3, functools.partial(cumsum_matmul, nb)),
            (nb + l_soft + l_cum + l_w, 4, functools.partial(weights, nb)),
            (nb + l_soft, 5, functools.partial(softplus, nb)),
        ]
    _emit_in_order(events)


def _sb_attn(qkv, side, *, batch, seq, tile, heads_per_step):
    kern = functools.partial(_sb_kernel, seq=seq, tile=tile, heads=heads_per_step)
    groups = N_HEADS_SB // heads_per_step
    base = 3 * (N_HEADS_MOBA // heads_per_step)
    hq, hk, hv = base, base + groups, base + 2 * groups
    blk = (seq, heads_per_step * HEAD_DIM)
    side_in, side_out, side_shape = _side_cast_specs(
        side, batch * groups, lambda b, h: (b * groups + h, 0))
    return pl.pallas_call(
        kern,
        out_shape=(jax.ShapeDtypeStruct((batch * seq, N_HEADS_SB * HEAD_DIM), ATTN_OUT_DTYPE), side_shape),
        grid=(batch, groups),
        in_specs=[
            pl.BlockSpec(blk, lambda b, h: (b, hq + h)),
            pl.BlockSpec(blk, lambda b, h: (b, hk + h)),
            pl.BlockSpec(blk, lambda b, h: (b, hv + h)),
            side_in,
        ],
        out_specs=(pl.BlockSpec(blk, lambda b, h: (b, h)), side_out),
        scratch_shapes=[pltpu.VMEM((heads_per_step, HEAD_DIM, seq), BF16)],
        compiler_params=pltpu.CompilerParams(
            dimension_semantics=("arbitrary", "arbitrary"),
            vmem_limit_bytes=VMEM_LIMIT_BYTES),
        name="sb_attn",
    )(qkv, qkv, qkv, side)


def _out_proj_kernel(oa_ref, ob_ref, ga_ref, gb_ref, w_ref, x_ref, y_ref):
    d_a = oa_ref.shape[1]
    oa = oa_ref[...].astype(F32)
    ob = ob_ref[...].astype(F32)
    na = ((oa * _rms_scale(oa)) * ga_ref[...]).astype(BF16)
    nb = ((ob * _rms_scale(ob)) * gb_ref[...]).astype(BF16)
    cw = y_ref.shape[1] // EPILOGUE_CHUNKS
    for c in range(EPILOGUE_CHUNKS):
        cols = slice(c * cw, (c + 1) * cw)
        y = (jnp.dot(na, w_ref[0:d_a, cols], preferred_element_type=F32)
             + jnp.dot(nb, w_ref[d_a:, cols], preferred_element_type=F32))
        y_ref[:, cols] = x_ref[:, cols] + y


def _out_proj(o_a, o_b, g_a, g_b, w_bf16, x2, *, tm):
    m, d_a = o_a.shape
    d_b = o_b.shape[1]
    d = w_bf16.shape[1]
    return pl.pallas_call(
        _out_proj_kernel,
        out_shape=jax.ShapeDtypeStruct((m, d), F32),
        grid=(m // tm,),
        in_specs=[
            pl.BlockSpec((tm, d_a), lambda i: (i, 0)),
            pl.BlockSpec((tm, d_b), lambda i: (i, 0)),
            pl.BlockSpec((1, d_a), lambda i: (0, 0)),
            pl.BlockSpec((1, d_b), lambda i: (0, 0)),
            pl.BlockSpec((d_a + d_b, d), lambda i: (0, 0), pipeline_mode=pl.Buffered(1)),
            pl.BlockSpec((tm, d), lambda i: (i, 0)),
        ],
        out_specs=pl.BlockSpec((tm, d), lambda i: (i, 0)),
        compiler_params=pltpu.CompilerParams(
            dimension_semantics=("arbitrary",),
            vmem_limit_bytes=LARGE_VMEM_LIMIT_BYTES),
        name="out_proj",
    )(o_a, o_b, g_a, g_b, w_bf16, x2)


def _mlp_kernel(x_ref, g_ref, wu_ref, wd_ref, gf_ref, o_ref, h_ref, *, n_chunk):
    f = pl.program_id(1)
    n_f = pl.num_programs(1)

    cw = o_ref.shape[1] // n_chunk

    def mlp_slice(h, base):
        u = jnp.dot(h, wu_ref[...], preferred_element_type=F32)
        r = jnp.maximum(u, 0.0)
        act = (r * r).astype(BF16)
        for c in range(n_chunk):
            cols = slice(c * cw, (c + 1) * cw)
            o_ref[:, cols] = base(cols) + jnp.dot(act, wd_ref[:, cols], preferred_element_type=F32)

    @pl.when(f == 0)
    def _():
        x = x_ref[...]
        h = ((x * _rms_scale(x)) * g_ref[...]).astype(BF16)
        h_ref[...] = h
        mlp_slice(h, lambda cols: x_ref[:, cols])

    @pl.when((f > 0) & (f < n_f - 1))
    def _():
        mlp_slice(h_ref[...], lambda cols: o_ref[:, cols])

    @pl.when(f == n_f - 1)
    def _():
        u = jnp.dot(h_ref[...], wu_ref[...], preferred_element_type=F32)
        r = jnp.maximum(u, 0.0)
        act = (r * r).astype(BF16)
        rh = o_ref.shape[0] // n_chunk
        for rc in range(n_chunk):
            rows = slice(rc * rh, (rc + 1) * rh)
            y = o_ref[rows, :] + jnp.dot(act[rows, :], wd_ref[...], preferred_element_type=F32)
            o_ref[rows, :] = (y * _rms_scale(y)) * gf_ref[...]


def _mlp(x1, g, wu_bf16, wd_bf16, g_final, *, tm, tf):
    m, d = x1.shape
    d_ff = wu_bf16.shape[1]
    assert d_ff // tf >= 2, "first and last d_ff slices take different code paths"
    kern = functools.partial(_mlp_kernel, n_chunk=EPILOGUE_CHUNKS)
    return pl.pallas_call(
        kern,
        out_shape=jax.ShapeDtypeStruct((m, d), F32),
        grid=(m // tm, d_ff // tf),
        in_specs=[
            pl.BlockSpec((tm, d), lambda i, f: (i, 0)),
            pl.BlockSpec((1, d), lambda i, f: (0, 0)),
            pl.BlockSpec((d, tf), lambda i, f: (0, f)),
            pl.BlockSpec((tf, d), lambda i, f: (f, 0)),
            pl.BlockSpec((1, d), lambda i, f: (0, 0)),
        ],
        out_specs=pl.BlockSpec((tm, d), lambda i, f: (i, 0)),
        scratch_shapes=[pltpu.VMEM((tm, d), BF16)],
        compiler_params=pltpu.CompilerParams(
            dimension_semantics=("arbitrary", "arbitrary"),
            vmem_limit_bytes=LARGE_VMEM_LIMIT_BYTES),
        name="mlp",
    )(x1, g, wu_bf16, wd_bf16, g_final)


def _rope_tables(seq):
    half = ROPE_DIMS // 2
    inv_freq = ROPE_THETA ** (-np.arange(half, dtype=np.float64) / half)
    ang = np.arange(seq, dtype=np.float64)[:, None] * inv_freq[None, :]
    cos, sin = np.cos(ang), np.sin(ang)
    ones = np.ones((seq, HEAD_DIM - ROPE_DIMS))
    zeros_rest = np.zeros((seq, HEAD_DIM - ROPE_DIMS))
    cos_t = np.concatenate([cos, cos, ones], axis=1)
    sin_t = np.concatenate([-sin, sin, zeros_rest], axis=1)
    return jnp.asarray(cos_t, F32), jnp.asarray(sin_t, F32)


def kernel(x, mix_norm_g, w_in, moba_out_g, sb_out_g, w_out, mlp_norm_g, w_up, w_down, final_norm_g):
    batch, seq, d_model = x.shape
    depth = w_in.shape[0]
    cos_t, sin_t = _rope_tables(seq)
    x2 = x.reshape(batch * seq, d_model)
    for l in range(depth):
        qkv = _qkv_proj(x2, mix_norm_g[l][None, :], w_in[l], cos_t, sin_t,
                        seq=seq, tm=QKV_ROW_TILE, tn=QKV_COL_TILE, n_groups=QKV_ROW_GROUPS)
        o_a, w_up_bf, w_out_bf = _moba_attn(qkv, w_up[l], w_out[l], batch=batch, seq=seq,
                                            heads_per_step=ATTN_HEADS_PER_STEP)
        o_b, w_down_bf = _sb_attn(qkv, w_down[l], batch=batch, seq=seq, tile=SB_TILE,
                                  heads_per_step=ATTN_HEADS_PER_STEP)
        x1 = _out_proj(o_a, o_b, moba_out_g[l][None, :], sb_out_g[l][None, :],
                       w_out_bf, x2, tm=OUT_PROJ_ROW_TILE)
        last = l == depth - 1
        assert last, "kernel fuses the final RMSNorm into the last layer's MLP; DEPTH must be 1"
        x2 = _mlp(x1, mlp_norm_g[l][None, :], w_up_bf, w_down_bf,
                  final_norm_g[None, :], tm=MLP_ROW_TILE, tf=MLP_FF_TILE)
    return x2.reshape(batch, seq, d_model)
```

```python
import functools
import math

import jax
import jax.numpy as jnp
import numpy as np
from jax import lax
from jax.experimental import pallas as pl
from jax.experimental.pallas import tpu as pltpu

HEAD_DIM = 128
N_HEADS_MOBA = 8
N_HEADS_SB = 8
MOBA_BLOCK = 256
MOBA_TOPK = 3
ROPE_THETA = 500000.0
ROPE_DIMS = HEAD_DIM // 4
EPS = 1e-6
NEG = -1e30

F32 = jnp.float32
BF16 = jnp.bfloat16
ATTN_OUT_DTYPE = BF16
ATTN_HEADS_PER_STEP = 2

_NT = (((1,), (1,)), ((), ()))

VMEM_LIMIT_BYTES = 56 * 1024 * 1024
LARGE_VMEM_LIMIT_BYTES = 62 * 1024 * 1024

QKV_ROW_TILE, QKV_COL_TILE, QKV_ROW_GROUPS = 1024, 1024, 2
OUT_PROJ_ROW_TILE = 1024
MLP_ROW_TILE, MLP_FF_TILE = 1024, 1024
SB_TILE = 256
EPILOGUE_CHUNKS = 4

Q_SCALE = HEAD_DIM ** -0.5 * math.log2(math.e)


def _rms_scale(x):
    return lax.rsqrt(jnp.mean(x * x, axis=-1, keepdims=True) + EPS)


def _emit_in_order(events):
    for _, _, thunk in sorted(events, key=lambda e: (e[0], e[1])):
        thunk()


def _qkv_kernel(x_ref, g_ref, w_ref, cos_ref, sin_ref, o_ref, h_ref):
    j = pl.program_id(1)
    i = pl.program_id(2)
    tm = x_ref.shape[0]
    rows = pl.ds(pl.multiple_of(i * tm, tm), tm)

    def project(h=None):
        h = h_ref[rows, :] if h is None else h
        return jnp.dot(h, w_ref[...].astype(BF16), preferred_element_type=F32)

    def rope(y, post_scale):
        n_heads = y.shape[1] // HEAD_DIM
        half = ROPE_DIMS // 2
        lane = lax.broadcasted_iota(jnp.int32, (y.shape[0], HEAD_DIM), 1)
        partner = jnp.where(lane < ROPE_DIMS, lane ^ half, lane)
        for hd in range(n_heads):
            t = y[:, hd * HEAD_DIM:(hd + 1) * HEAD_DIM]
            swapped = jnp.take_along_axis(t, partner, axis=1)
            r = t * cos_ref[...] + swapped * sin_ref[...]
            if post_scale is not None:
                r = r * post_scale
            o_ref[:, hd * HEAD_DIM:(hd + 1) * HEAD_DIM] = r.astype(o_ref.dtype)

    @pl.when(j == 0)
    def _():
        x = x_ref[...]
        h = ((x * _rms_scale(x)) * g_ref[...]).astype(BF16)
        h_ref[rows, :] = h
        rope(project(h), Q_SCALE)

    @pl.when(j == 1)
    def _():
        rope(project(), None)

    @pl.when(j == 3)
    def _():
        o_ref[...] = (project() * Q_SCALE).astype(o_ref.dtype)

    @pl.when((j == 2) | (j >= 4))
    def _():
        o_ref[...] = project().astype(o_ref.dtype)


def _side_cast_specs(side, n_steps, index_map):
    rows, cols = side.shape
    slab = rows // n_steps
    assert slab * n_steps == rows and slab % 16 == 0, (rows, n_steps)
    spec = pl.BlockSpec((slab, cols), index_map)
    return spec, spec, jax.ShapeDtypeStruct((rows, cols), BF16)


def _qkv_proj(x2, g, w, cos_t, sin_t, *, seq, tm, tn, n_groups):
    m, d = x2.shape
    n = w.shape[1]
    assert tn == N_HEADS_MOBA * HEAD_DIM == N_HEADS_SB * HEAD_DIM and n == 6 * tn, (tn, n)
    tiles = m // (tm * n_groups)
    pos_blocks = seq // tm
    n_rope_tiles = 2

    def x_tile(gr, j, i):
        return gr * tiles + jnp.where(j == 0, i, tiles - 1)

    tab_spec = pl.BlockSpec(
        (tm, HEAD_DIM), lambda gr, j, i: (jnp.where(j < n_rope_tiles, (gr * tiles + i) % pos_blocks, 0), 0))
    return pl.pallas_call(
        _qkv_kernel,
        out_shape=jax.ShapeDtypeStruct((m, n), BF16),
        grid=(n_groups, n // tn, tiles),
        in_specs=[
            pl.BlockSpec((tm, d), lambda gr, j, i: (x_tile(gr, j, i), 0)),
            pl.BlockSpec((1, d), lambda gr, j, i: (0, 0)),
            pl.BlockSpec((d, tn), lambda gr, j, i: (0, j)),
            tab_spec, tab_spec,
        ],
        out_specs=pl.BlockSpec((tm, tn), lambda gr, j, i: (gr * tiles + i, j)),
        scratch_shapes=[pltpu.VMEM((tiles * tm, d), BF16)],
        compiler_params=pltpu.CompilerParams(
            dimension_semantics=("arbitrary", "arbitrary", "arbitrary"),
            vmem_limit_bytes=LARGE_VMEM_LIMIT_BYTES),
        name="qkv_proj",
    )(x2, g, w, cos_t, sin_t)


_MOBA_LAG_MASK, _MOBA_LAG_EXP, _MOBA_LAG_PV = 3, 3, 2
_MOBA_SUM_ROWS = 16


def _moba_kernel(q_ref, k_ref, v_ref, *refs, seq, heads, n_sides):
    side_refs, o_ref = refs[:n_sides], refs[n_sides]
    side_bf_refs, vt_ref = refs[n_sides + 1:2 * n_sides + 1], refs[2 * n_sides + 1]
    blk = MOBA_BLOCK
    n_blk = seq // blk

    key_i = lax.broadcasted_iota(jnp.int32, (blk, blk), 0)
    qry_i = lax.broadcasted_iota(jnp.int32, (blk, blk), 1)
    causal = key_i <= qry_i
    blk_id = lax.broadcasted_iota(jnp.int32, (n_blk, blk), 0)

    def head_cols(hd):
        return slice(hd * HEAD_DIM, (hd + 1) * HEAD_DIM)

    blocks = [(hd, i, j) for hd in range(heads)
              for i in range(n_blk - 1, -1, -1) for j in ([i] + list(range(i)))]
    last_of_tile = {i: (i - 1 if i else 0) for i in range(n_blk)}
    km_rows, raw, gate_raw, bias, scores, col_max, p_bf, acc = {}, {}, {}, {}, {}, {}, {}, {}

    def mean_keys(hd):
        kf = k_ref[:, head_cols(hd)].astype(F32)
        k_mean = jnp.concatenate(
            [jnp.mean(kf[n * blk:(n + 1) * blk, :], axis=0, keepdims=True) for n in range(n_blk)], axis=0)
        km_hi = k_mean.astype(BF16)
        km_lo = (k_mean - km_hi.astype(F32)).astype(BF16)
        km_rows[hd] = jnp.concatenate([km_hi, km_lo], axis=0)

    def transpose_values(hd):
        vt_ref[hd, 0:HEAD_DIM, :] = v_ref[:, head_cols(hd)].astype(F32).T.astype(BF16)
        extra = lax.broadcasted_iota(jnp.int32, (_MOBA_SUM_ROWS, seq), 0)
        vt_ref[hd, HEAD_DIM:, :] = (extra == 0).astype(BF16)

    def score_matmul(nb):
        hd, i, j = blocks[nb]
        q_i = q_ref[i * blk:(i + 1) * blk, head_cols(hd)]
        k_j = k_ref[j * blk:(j + 1) * blk, head_cols(hd)]
        if j == i:
            r = lax.dot_general(jnp.concatenate([k_j, km_rows[hd]], axis=0), q_i, _NT,
                                preferred_element_type=F32)
            raw[nb] = r[0:blk, :]
            gate_raw[hd, i] = r[blk:blk + n_blk, :] + r[blk + n_blk:blk + 2 * n_blk, :]
        else:
            raw[nb] = lax.dot_general(k_j, q_i, _NT, preferred_element_type=F32)

    def select_blocks(hd, i):
        past = blk_id < i
        g = jnp.where(past, gate_raw.pop((hd, i)), NEG)
        rank = jnp.zeros((n_blk, blk), jnp.int32)
        for other in range(n_blk):
            g_o = g[other:other + 1, :]
            beats = (g_o > g) | ((g_o == g) & (other < blk_id))
            rank = rank + beats.astype(jnp.int32)
        bias[hd, i] = jnp.where(past & (rank < MOBA_TOPK), 0.0, NEG).astype(F32)

    def mask_and_max(nb):
        hd, i, j = blocks[nb]
        s = raw.pop(nb)
        if j == i:
            s = jnp.where(causal, s, NEG)
            select_blocks(hd, i)
        else:
            s = s + bias[hd, i][j:j + 1, :]
        scores[nb] = s
        cm = jnp.max(s, axis=0, keepdims=True)
        col_max[hd, i] = cm if j == i else jnp.maximum(col_max[hd, i], cm)

    def exponentiate(nb):
        hd, i, j = blocks[nb]
        p = jnp.exp2(scores.pop(nb) - col_max[hd, i])
        p_bf[nb] = p.astype(BF16)

    def value_matmul(nb):
        hd, i, j = blocks[nb]
        d = jnp.dot(vt_ref[hd, :, j * blk:(j + 1) * blk], p_bf.pop(nb), preferred_element_type=F32)
        acc[hd, i] = d if j == i else acc[hd, i] + d
        if j == last_of_tile[i]:
            total = acc.pop((hd, i))
            out = (total[0:HEAD_DIM, :] / total[HEAD_DIM:HEAD_DIM + 1, :]).T
            o_ref[i * blk:(i + 1) * blk, head_cols(hd)] = out.astype(o_ref.dtype)

    def cast_side():
        for src, dst in zip(side_refs, side_bf_refs):
            dst[...] = src[...].astype(BF16)

    mask_step = [nb + _MOBA_LAG_MASK for nb in range(len(blocks))]
    tile_done = {}
    for nb, (hd, i, _) in enumerate(blocks):
        tile_done[hd, i] = max(tile_done.get((hd, i), 0), mask_step[nb])
    per_head = len(blocks) // heads
    events = [(2, 1, cast_side)]
    for hd in range(heads):
        first = hd * per_head
        events += [(max(first - _MOBA_LAG_MASK, 0), -1, functools.partial(mean_keys, hd)),
                   (first + 1, 1, functools.partial(transpose_values, hd))]
    exp_step = -1
    for nb, (hd, i, _) in enumerate(blocks):
        exp_step = max(exp_step + 1, tile_done[hd, i] + _MOBA_LAG_EXP)
        events += [
            (nb, 0, functools.partial(score_matmul, nb)),
            (exp_step + _MOBA_LAG_PV, 2, functools.partial(value_matmul, nb)),
            (exp_step, 3, functools.partial(exponentiate, nb)),
            (mask_step[nb], 4, functools.partial(mask_and_max, nb)),
        ]
    _emit_in_order(events)


def _moba_attn(qkv, sides, *, batch, seq, heads_per_step):
    kern = functools.partial(_moba_kernel, seq=seq, heads=heads_per_step, n_sides=len(sides))
    groups = N_HEADS_MOBA // heads_per_step
    hq, hk, hv = 0, groups, 2 * groups
    blk = (seq, heads_per_step * HEAD_DIM)
    side_specs = [_side_cast_specs(s, batch * groups, lambda b, h: (b * groups + h, 0)) for s in sides]
    return pl.pallas_call(
        kern,
        out_shape=(jax.ShapeDtypeStruct((batch * seq, N_HEADS_MOBA * HEAD_DIM), ATTN_OUT_DTYPE),
                   *[s[2] for s in side_specs]),
        grid=(batch, groups),
        in_specs=[
            pl.BlockSpec(blk, lambda b, h: (b, hq + h)),
            pl.BlockSpec(blk, lambda b, h: (b, hk + h)),
            pl.BlockSpec(blk, lambda b, h: (b, hv + h)),
            *[s[0] for s in side_specs],
        ],
        out_specs=(pl.BlockSpec(blk, lambda b, h: (b, h)), *[s[1] for s in side_specs]),
        scratch_shapes=[pltpu.VMEM((heads_per_step, HEAD_DIM + _MOBA_SUM_ROWS, seq), BF16)],
        compiler_params=pltpu.CompilerParams(
            dimension_semantics=("arbitrary", "arbitrary"),
            vmem_limit_bytes=VMEM_LIMIT_BYTES),
        name="moba_attn",
    )(qkv, qkv, qkv, *sides)


_SB_LAGS = (1, 1, 2, 1)
_LOG2E_HI = float(np.asarray(math.log2(math.e), dtype=jnp.bfloat16))
_LOG2E_LO = float(np.asarray(math.log2(math.e) - _LOG2E_HI, dtype=jnp.bfloat16))


def _sb_kernel(q_ref, k_ref, v_ref, *refs, seq, tile, heads, n_sides):
    side_refs, o_ref = refs[:n_sides], refs[n_sides]
    side_bf_refs, vt_ref = refs[n_sides + 1:2 * n_sides + 1], refs[2 * n_sides + 1]
    n_tiles = seq // tile
    key_i = lax.broadcasted_iota(jnp.int32, (tile, tile), 0)
    qry_i = lax.broadcasted_iota(jnp.int32, (tile, tile), 1)
    causal = key_i < qry_i
    this_or_later = (qry_i >= key_i).astype(BF16)

    def head_cols(hd):
        return slice(hd * HEAD_DIM, (hd + 1) * HEAD_DIM)

    blocks = [(hd, i, j) for hd in range(heads)
              for i in range(n_tiles - 1, -1, -1) for j in range(i, -1, -1)]
    raw, logit, soft_bf, later_sum, a_bf, acc, carry = {}, {}, {}, {}, {}, {}, {}

    def logit_matmul(nb):
        hd, i, j = blocks[nb]
        raw[nb] = lax.dot_general(k_ref[j * tile:(j + 1) * tile, head_cols(hd)],
                                  q_ref[i * tile:(i + 1) * tile, head_cols(hd)],
                                  _NT, preferred_element_type=F32)

    def softplus(nb):
        _, i, j = blocks[nb]
        z = raw.pop(nb)
        zb = z.astype(BF16)
        ln_term = jnp.log(1.0 + jnp.exp2(-jnp.abs(zb)))
        t = jnp.maximum(zb, 0.0) + (ln_term * _LOG2E_HI + ln_term * _LOG2E_LO)
        if j == i:
            t = jnp.where(causal, t, 0.0)
        logit[nb] = z
        soft_bf[nb] = t.astype(BF16)

    def cumsum_matmul(nb):
        later_sum[nb] = jnp.dot(this_or_later, soft_bf.pop(nb), preferred_element_type=F32)

    def weights(nb):
        hd, i, j = blocks[nb]
        inc = later_sum.pop(nb)
        x = logit.pop(nb) - inc
        if j != i:
            x = x - carry[hd, i]
        a = jnp.exp2(x)
        if j == i:
            a = jnp.where(causal, a, 0.0)
        total = inc[0:1, :]
        carry[hd, i] = total if j == i else carry[hd, i] + total
        a_bf[nb] = a.astype(BF16)

    def value_matmul(nb):
        hd, i, j = blocks[nb]
        d = jnp.dot(vt_ref[hd, :, j * tile:(j + 1) * tile], a_bf.pop(nb), preferred_element_type=F32)
        acc[hd, i] = d if j == i else acc[hd, i] + d
        if j == 0:
            o_ref[i * tile:(i + 1) * tile, head_cols(hd)] = acc.pop((hd, i)).T.astype(o_ref.dtype)

    def transpose_values(hd):
        vt_ref[hd] = v_ref[:, head_cols(hd)].astype(F32).T.astype(BF16)

    def cast_side():
        for src, dst in zip(side_refs, side_bf_refs):
            dst[...] = src[...].astype(BF16)

    l_soft, l_cum, l_w, l_pv = _SB_LAGS
    per_head = len(blocks) // heads
    events = [(2, 1, cast_side)]
    events += [(hd * per_head + 1, 1, functools.partial(transpose_values, hd)) for hd in range(heads)]
    for nb in range(len(blocks)):
        events += [
            (nb, 0, functools.partial(logit_matmul, nb)),
            (nb + l_soft + l_cum + l_w + l_pv, 2, functools.partial(value_matmul, nb)),
            (nb + l_soft + l_cum, 3, functools.partial(cumsum_matmul, nb)),
            (nb + l_soft + l_cum + l_w, 4, functools.partial(weights, nb)),
            (nb + l_soft, 5, functools.partial(softplus, nb)),
        ]
    _emit_in_order(events)


def _sb_attn(qkv, sides, *, batch, seq, tile, heads_per_step):
    kern = functools.partial(_sb_kernel, seq=seq, tile=tile, heads=heads_per_step, n_sides=len(sides))
    groups = N_HEADS_SB // heads_per_step
    base = 3 * (N_HEADS_MOBA // heads_per_step)
    hq, hk, hv = base, base + groups, base + 2 * groups
    blk = (seq, heads_per_step * HEAD_DIM)
    side_specs = [_side_cast_specs(s, batch * groups, lambda b, h: (b * groups + h, 0)) for s in sides]
    return pl.pallas_call(
        kern,
        out_shape=(jax.ShapeDtypeStruct((batch * seq, N_HEADS_SB * HEAD_DIM), ATTN_OUT_DTYPE),
                   *[s[2] for s in side_specs]),
        grid=(batch, groups),
        in_specs=[
            pl.BlockSpec(blk, lambda b, h: (b, hq + h)),
            pl.BlockSpec(blk, lambda b, h: (b, hk + h)),
            pl.BlockSpec(blk, lambda b, h: (b, hv + h)),
            *[s[0] for s in side_specs],
        ],
        out_specs=(pl.BlockSpec(blk, lambda b, h: (b, h)), *[s[1] for s in side_specs]),
        scratch_shapes=[pltpu.VMEM((heads_per_step, HEAD_DIM, seq), BF16)],
        compiler_params=pltpu.CompilerParams(
            dimension_semantics=("arbitrary", "arbitrary"),
            vmem_limit_bytes=VMEM_LIMIT_BYTES),
        name="sb_attn",
    )(qkv, qkv, qkv, *sides)


def _out_proj_kernel(oa_ref, ob_ref, ga_ref, gb_ref, w_ref, x_ref, y_ref):
    d_a = oa_ref.shape[1]
    oa = oa_ref[...].astype(F32)
    ob = ob_ref[...].astype(F32)
    na = ((oa * _rms_scale(oa)) * ga_ref[...]).astype(BF16)
    nb = ((ob * _rms_scale(ob)) * gb_ref[...]).astype(BF16)
    cw = y_ref.shape[1] // EPILOGUE_CHUNKS
    for c in range(EPILOGUE_CHUNKS):
        cols = slice(c * cw, (c + 1) * cw)
        y = (jnp.dot(na, w_ref[0:d_a, cols], preferred_element_type=F32)
             + jnp.dot(nb, w_ref[d_a:, cols], preferred_element_type=F32))
        y_ref[:, cols] = x_ref[:, cols] + y


def _out_proj(o_a, o_b, g_a, g_b, w_bf16, x2, *, tm):
    m, d_a = o_a.shape
    d_b = o_b.shape[1]
    d = w_bf16.shape[1]
    return pl.pallas_call(
        _out_proj_kernel,
        out_shape=jax.ShapeDtypeStruct((m, d), F32),
        grid=(m // tm,),
        in_specs=[
            pl.BlockSpec((tm, d_a), lambda i: (i, 0)),
            pl.BlockSpec((tm, d_b), lambda i: (i, 0)),
            pl.BlockSpec((1, d_a), lambda i: (0, 0)),
            pl.BlockSpec((1, d_b), lambda i: (0, 0)),
            pl.BlockSpec((d_a + d_b, d), lambda i: (0, 0), pipeline_mode=pl.Buffered(1)),
            pl.BlockSpec((tm, d), lambda i: (i, 0)),
        ],
        out_specs=pl.BlockSpec((tm, d), lambda i: (i, 0)),
        compiler_params=pltpu.CompilerParams(
            dimension_semantics=("arbitrary",),
            vmem_limit_bytes=LARGE_VMEM_LIMIT_BYTES),
        name="out_proj",
    )(o_a, o_b, g_a, g_b, w_bf16, x2)


def _mlp_kernel(x_ref, g_ref, wu_ref, wd_ref, gf_ref, o_ref, h_ref, *, n_chunk):
    f = pl.program_id(1)
    n_f = pl.num_programs(1)

    cw = o_ref.shape[1] // n_chunk

    def mlp_slice(h, base):
        u = jnp.dot(h, wu_ref[...], preferred_element_type=F32)
        r = jnp.maximum(u, 0.0)
        act = (r * r).astype(BF16)
        for c in range(n_chunk):
            cols = slice(c * cw, (c + 1) * cw)
            o_ref[:, cols] = base(cols) + jnp.dot(act, wd_ref[:, cols], preferred_element_type=F32)

    @pl.when(f == 0)
    def _():
        x = x_ref[...]
        h = ((x * _rms_scale(x)) * g_ref[...]).astype(BF16)
        h_ref[...] = h
        mlp_slice(h, lambda cols: x_ref[:, cols])

    @pl.when((f > 0) & (f < n_f - 1))
    def _():
        mlp_slice(h_ref[...], lambda cols: o_ref[:, cols])

    @pl.when(f == n_f - 1)
    def _():
        u = jnp.dot(h_ref[...], wu_ref[...], preferred_element_type=F32)
        r = jnp.maximum(u, 0.0)
        act = (r * r).astype(BF16)
        rh = o_ref.shape[0] // n_chunk
        for rc in range(n_chunk):
            rows = slice(rc * rh, (rc + 1) * rh)
            y = o_ref[rows, :] + jnp.dot(act[rows, :], wd_ref[...], preferred_element_type=F32)
            o_ref[rows, :] = (y * _rms_scale(y)) * gf_ref[...]


def _mlp(x1, g, wu_bf16, wd_bf16, g_final, *, tm, tf):
    m, d = x1.shape
    d_ff = wu_bf16.shape[1]
    assert d_ff // tf >= 2, "first and last d_ff slices take different code paths"
    kern = functools.partial(_mlp_kernel, n_chunk=EPILOGUE_CHUNKS)
    return pl.pallas_call(
        kern,
        out_shape=jax.ShapeDtypeStruct((m, d), F32),
        grid=(m // tm, d_ff // tf),
        in_specs=[
            pl.BlockSpec((tm, d), lambda i, f: (i, 0)),
            pl.BlockSpec((1, d), lambda i, f: (0, 0)),
            pl.BlockSpec((d, tf), lambda i, f: (0, f)),
            pl.BlockSpec((tf, d), lambda i, f: (f, 0)),
            pl.BlockSpec((1, d), lambda i, f: (0, 0)),
        ],
        out_specs=pl.BlockSpec((tm, d), lambda i, f: (i, 0)),
        scratch_shapes=[pltpu.VMEM((tm, d), BF16)],
        compiler_params=pltpu.CompilerParams(
            dimension_semantics=("arbitrary", "arbitrary"),
            vmem_limit_bytes=LARGE_VMEM_LIMIT_BYTES),
        name="mlp",
    )(x1, g, wu_bf16, wd_bf16, g_final)


def _rope_tables(seq):
    half = ROPE_DIMS // 2
    inv_freq = ROPE_THETA ** (-np.arange(half, dtype=np.float64) / half)
    ang = np.arange(seq, dtype=np.float64)[:, None] * inv_freq[None, :]
    cos, sin = np.cos(ang), np.sin(ang)
    ones = np.ones((seq, HEAD_DIM - ROPE_DIMS))
    zeros_rest = np.zeros((seq, HEAD_DIM - ROPE_DIMS))
    cos_t = np.concatenate([cos, cos, ones], axis=1)
    sin_t = np.concatenate([-sin, sin, zeros_rest], axis=1)
    return jnp.asarray(cos_t, F32), jnp.asarray(sin_t, F32)


def kernel(x, mix_norm_g, w_in, moba_out_g, sb_out_g, w_out, mlp_norm_g, w_up, w_down, final_norm_g):
    batch, seq, d_model = x.shape
    depth = w_in.shape[0]
    cos_t, sin_t = _rope_tables(seq)
    x2 = x.reshape(batch * seq, d_model)
    for l in range(depth):
        qkv = _qkv_proj(x2, mix_norm_g[l][None, :], w_in[l], cos_t, sin_t,
                        seq=seq, tm=QKV_ROW_TILE, tn=QKV_COL_TILE, n_groups=QKV_ROW_GROUPS)
        o_a, w_up_bf, w_out_bf = _moba_attn(qkv, [w_up[l], w_out[l]], batch=batch, seq=seq,
                                            heads_per_step=ATTN_HEADS_PER_STEP)
        o_b, w_down_bf = _sb_attn(qkv, [w_down[l]], batch=batch, seq=seq, tile=SB_TILE,
                                  heads_per_step=ATTN_HEADS_PER_STEP)
        x1 = _out_proj(o_a, o_b, moba_out_g[l][None, :], sb_out_g[l][None, :],
                       w_out_bf, x2, tm=OUT_PROJ_ROW_TILE)
        last = l == depth - 1
        assert last, "kernel fuses the final RMSNorm into the last layer's MLP; DEPTH must be 1"
        x2 = _mlp(x1, mlp_norm_g[l][None, :], w_up_bf, w_down_bf,
                  final_norm_g[None, :], tm=MLP_ROW_TILE, tf=MLP_FF_TILE)
    return x2.reshape(batch, seq, d_model)
```

```python
import functools
import math

import jax
import jax.numpy as jnp
import numpy as np
from jax import lax
from jax.experimental import pallas as pl
from jax.experimental.pallas import tpu as pltpu

HEAD_DIM = 128
N_HEADS_MOBA = 8
N_HEADS_SB = 8
MOBA_BLOCK = 256
MOBA_TOPK = 3
ROPE_THETA = 500000.0
ROPE_DIMS = HEAD_DIM // 4
EPS = 1e-6
NEG = -1e30

F32 = jnp.float32
BF16 = jnp.bfloat16
ATTN_OUT_DTYPE = BF16
ATTN_HEADS_PER_STEP = 2

_NT = (((1,), (1,)), ((), ()))

VMEM_LIMIT_BYTES = 56 * 1024 * 1024
LARGE_VMEM_LIMIT_BYTES = 62 * 1024 * 1024

QKV_ROW_TILE, QKV_COL_TILE, QKV_ROW_GROUPS = 1024, 1024, 2
OUT_PROJ_ROW_TILE = 1024
MLP_ROW_TILE, MLP_FF_TILE = 1024, 1024
SB_TILE = 256
EPILOGUE_CHUNKS = 4

Q_SCALE = HEAD_DIM ** -0.5 * math.log2(math.e)


def _rms_scale(x):
    return lax.rsqrt(jnp.mean(x * x, axis=-1, keepdims=True) + EPS)


def _emit_in_order(events):
    for _, _, thunk in sorted(events, key=lambda e: (e[0], e[1])):
        thunk()


def _qkv_kernel(x_ref, g_ref, w_ref, cos_ref, sin_ref, o_ref, h_ref):
    j = pl.program_id(1)
    i = pl.program_id(2)
    tm = x_ref.shape[0]
    rows = pl.ds(pl.multiple_of(i * tm, tm), tm)

    def project(h=None):
        h = h_ref[rows, :] if h is None else h
        return jnp.dot(h, w_ref[...].astype(BF16), preferred_element_type=F32)

    def rope(y, post_scale):
        n_heads = y.shape[1] // HEAD_DIM
        half = ROPE_DIMS // 2
        lane = lax.broadcasted_iota(jnp.int32, (y.shape[0], HEAD_DIM), 1)
        partner = jnp.where(lane < ROPE_DIMS, lane ^ half, lane)
        for hd in range(n_heads):
            t = y[:, hd * HEAD_DIM:(hd + 1) * HEAD_DIM]
            swapped = jnp.take_along_axis(t, partner, axis=1)
            r = t * cos_ref[...] + swapped * sin_ref[...]
            if post_scale is not None:
                r = r * post_scale
            o_ref[:, hd * HEAD_DIM:(hd + 1) * HEAD_DIM] = r.astype(o_ref.dtype)

    @pl.when(j == 0)
    def _():
        x = x_ref[...]
        h = ((x * _rms_scale(x)) * g_ref[...]).astype(BF16)
        h_ref[rows, :] = h
        rope(project(h), Q_SCALE)

    @pl.when(j == 1)
    def _():
        rope(project(), None)

    @pl.when(j == 3)
    def _():
        o_ref[...] = (project() * Q_SCALE).astype(o_ref.dtype)

    @pl.when((j == 2) | (j >= 4))
    def _():
        o_ref[...] = project().astype(o_ref.dtype)


def _side_cast_specs(side, n_steps, index_map):
    rows, cols = side.shape
    slab = rows // n_steps
    assert slab * n_steps == rows and slab % 16 == 0, (rows, n_steps)
    spec = pl.BlockSpec((slab, cols), index_map)
    return spec, spec, jax.ShapeDtypeStruct((rows, cols), BF16)


def _qkv_proj(x2, g, w, cos_t, sin_t, *, seq, tm, tn, n_groups):
    m, d = x2.shape
    n = w.shape[1]
    assert tn == N_HEADS_MOBA * HEAD_DIM == N_HEADS_SB * HEAD_DIM and n == 6 * tn, (tn, n)
    tiles = m // (tm * n_groups)
    pos_blocks = seq // tm
    n_rope_tiles = 2

    def x_tile(gr, j, i):
        return gr * tiles + jnp.where(j == 0, i, tiles - 1)

    tab_spec = pl.BlockSpec(
        (tm, HEAD_DIM), lambda gr, j, i: (jnp.where(j < n_rope_tiles, (gr * tiles + i) % pos_blocks, 0), 0))
    return pl.pallas_call(
        _qkv_kernel,
        out_shape=jax.ShapeDtypeStruct((m, n), BF16),
        grid=(n_groups, n // tn, tiles),
        in_specs=[
            pl.BlockSpec((tm, d), lambda gr, j, i: (x_tile(gr, j, i), 0)),
            pl.BlockSpec((1, d), lambda gr, j, i: (0, 0)),
            pl.BlockSpec((d, tn), lambda gr, j, i: (0, j)),
            tab_spec, tab_spec,
        ],
        out_specs=pl.BlockSpec((tm, tn), lambda gr, j, i: (gr * tiles + i, j)),
        scratch_shapes=[pltpu.VMEM((tiles * tm, d), BF16)],
        compiler_params=pltpu.CompilerParams(
            dimension_semantics=("arbitrary", "arbitrary", "arbitrary"),
            vmem_limit_bytes=LARGE_VMEM_LIMIT_BYTES),
        name="qkv_proj",
    )(x2, g, w, cos_t, sin_t)


_MOBA_LAG_MASK, _MOBA_LAG_EXP, _MOBA_LAG_PV = 4, 3, 2
_MOBA_SUM_ROWS = 16
_MOBA_SCORE_RING = 16


def _moba_kernel(q_ref, k_ref, v_ref, *refs, seq, heads, n_sides):
    side_refs, o_ref = refs[:n_sides], refs[n_sides]
    side_bf_refs, vt_ref, score_ref = refs[n_sides + 1:2 * n_sides + 1], refs[2 * n_sides + 1], refs[2 * n_sides + 2]
    blk = MOBA_BLOCK
    n_blk = seq // blk

    key_i = lax.broadcasted_iota(jnp.int32, (blk, blk), 0)
    qry_i = lax.broadcasted_iota(jnp.int32, (blk, blk), 1)
    causal = key_i <= qry_i
    blk_id = lax.broadcasted_iota(jnp.int32, (n_blk, blk), 0)

    def head_cols(hd):
        return slice(hd * HEAD_DIM, (hd + 1) * HEAD_DIM)

    blocks = [(hd, i, j) for hd in range(heads)
              for i in range(n_blk - 1, -1, -1) for j in ([i] + list(range(i)))]
    last_of_tile = {i: (i - 1 if i else 0) for i in range(n_blk)}
    km_rows, raw, gate_raw, bias, col_max, p_bf, acc = {}, {}, {}, {}, {}, {}, {}

    def mean_keys(hd):
        kf = k_ref[:, head_cols(hd)].astype(F32)
        k_mean = jnp.concatenate(
            [jnp.mean(kf[n * blk:(n + 1) * blk, :], axis=0, keepdims=True) for n in range(n_blk)], axis=0)
        km_hi = k_mean.astype(BF16)
        km_lo = (k_mean - km_hi.astype(F32)).astype(BF16)
        km_rows[hd] = jnp.concatenate([km_hi, km_lo], axis=0)

    def transpose_values(hd):
        vt_ref[hd, 0:HEAD_DIM, :] = v_ref[:, head_cols(hd)].astype(F32).T.astype(BF16)
        extra = lax.broadcasted_iota(jnp.int32, (_MOBA_SUM_ROWS, seq), 0)
        vt_ref[hd, HEAD_DIM:, :] = (extra == 0).astype(BF16)

    def score_matmul(nb):
        hd, i, j = blocks[nb]
        q_i = q_ref[i * blk:(i + 1) * blk, head_cols(hd)]
        k_j = k_ref[j * blk:(j + 1) * blk, head_cols(hd)]
        if j == i:
            r = lax.dot_general(jnp.concatenate([k_j, km_rows[hd]], axis=0), q_i, _NT,
                                preferred_element_type=F32)
            raw[nb] = r[0:blk, :]
            gate_raw[hd, i] = r[blk:blk + n_blk, :] + r[blk + n_blk:blk + 2 * n_blk, :]
        else:
            raw[nb] = lax.dot_general(k_j, q_i, _NT, preferred_element_type=F32)

    def select_blocks(hd, i):
        past = blk_id < i
        g = jnp.where(past, gate_raw.pop((hd, i)), NEG)
        rank = jnp.zeros((n_blk, blk), jnp.int32)
        for other in range(n_blk):
            g_o = g[other:other + 1, :]
            beats = (g_o > g) | ((g_o == g) & (other < blk_id))
            rank = rank + beats.astype(jnp.int32)
        bias[hd, i] = jnp.where(past & (rank < MOBA_TOPK), 0.0, NEG).astype(F32)

    def mask_and_max(nb):
        hd, i, j = blocks[nb]
        s = raw.pop(nb)
        if j == i:
            s = jnp.where(causal, s, NEG)
            select_blocks(hd, i)
        else:
            s = s + bias[hd, i][j:j + 1, :]
        score_ref[nb % _MOBA_SCORE_RING] = s
        cm = jnp.max(s, axis=0, keepdims=True)
        col_max[hd, i] = cm if j == i else jnp.maximum(col_max[hd, i], cm)

    def exponentiate(nb):
        hd, i, j = blocks[nb]
        p = jnp.exp2(score_ref[nb % _MOBA_SCORE_RING] - col_max[hd, i])
        p_bf[nb] = p.astype(BF16)

    def value_matmul(nb):
        hd, i, j = blocks[nb]
        d = jnp.dot(vt_ref[hd, :, j * blk:(j + 1) * blk], p_bf.pop(nb), preferred_element_type=F32)
        acc[hd, i] = d if j == i else acc[hd, i] + d
        if j == last_of_tile[i]:
            total = acc.pop((hd, i))
            out = (total[0:HEAD_DIM, :] / total[HEAD_DIM:HEAD_DIM + 1, :]).T
            o_ref[i * blk:(i + 1) * blk, head_cols(hd)] = out.astype(o_ref.dtype)

    def cast_side():
        for src, dst in zip(side_refs, side_bf_refs):
            dst[...] = src[...].astype(BF16)

    mask_step = [nb + _MOBA_LAG_MASK for nb in range(len(blocks))]
    tile_done = {}
    for nb, (hd, i, _) in enumerate(blocks):
        tile_done[hd, i] = max(tile_done.get((hd, i), 0), mask_step[nb])
    per_head = len(blocks) // heads
    events = [(2, 1, cast_side)]
    for hd in range(heads):
        first = hd * per_head
        events += [(max(first - _MOBA_LAG_MASK, 0), -1, functools.partial(mean_keys, hd)),
                   (first + 1, 1, functools.partial(transpose_values, hd))]
    exp_step = -1
    for nb, (hd, i, _) in enumerate(blocks):
        exp_step = max(exp_step + 1, tile_done[hd, i] + _MOBA_LAG_EXP)
        reuse = nb + _MOBA_SCORE_RING
        assert reuse >= len(blocks) or mask_step[reuse] >= exp_step, "score ring too small"
        events += [
            (nb, 0, functools.partial(score_matmul, nb)),
            (exp_step + _MOBA_LAG_PV, 2, functools.partial(value_matmul, nb)),
            (exp_step, 3, functools.partial(exponentiate, nb)),
            (mask_step[nb], 4, functools.partial(mask_and_max, nb)),
        ]
    _emit_in_order(events)


def _moba_attn(qkv, sides, *, batch, seq, heads_per_step):
    kern = functools.partial(_moba_kernel, seq=seq, heads=heads_per_step, n_sides=len(sides))
    groups = N_HEADS_MOBA // heads_per_step
    hq, hk, hv = 0, groups, 2 * groups
    blk = (seq, heads_per_step * HEAD_DIM)
    side_specs = [_side_cast_specs(s, batch * groups, lambda b, h: (b * groups + h, 0)) for s in sides]
    return pl.pallas_call(
        kern,
        out_shape=(jax.ShapeDtypeStruct((batch * seq, N_HEADS_MOBA * HEAD_DIM), ATTN_OUT_DTYPE),
                   *[s[2] for s in side_specs]),
        grid=(batch, groups),
        in_specs=[
            pl.BlockSpec(blk, lambda b, h: (b, hq + h)),
            pl.BlockSpec(blk, lambda b, h: (b, hk + h)),
            pl.BlockSpec(blk, lambda b, h: (b, hv + h)),
            *[s[0] for s in side_specs],
        ],
        out_specs=(pl.BlockSpec(blk, lambda b, h: (b, h)), *[s[1] for s in side_specs]),
        scratch_shapes=[pltpu.VMEM((heads_per_step, HEAD_DIM + _MOBA_SUM_ROWS, seq), BF16),
                        pltpu.VMEM((_MOBA_SCORE_RING, MOBA_BLOCK, MOBA_BLOCK), F32)],
        compiler_params=pltpu.CompilerParams(
            dimension_semantics=("arbitrary", "arbitrary"),
            vmem_limit_bytes=VMEM_LIMIT_BYTES),
        name="moba_attn",
    )(qkv, qkv, qkv, *sides)


_SB_LAGS = (1, 1, 2, 1)
_LOG2E_HI = float(np.asarray(math.log2(math.e), dtype=jnp.bfloat16))
_LOG2E_LO = float(np.asarray(math.log2(math.e) - _LOG2E_HI, dtype=jnp.bfloat16))


def _sb_kernel(q_ref, k_ref, v_ref, *refs, seq, tile, heads, n_sides):
    side_refs, o_ref = refs[:n_sides], refs[n_sides]
    side_bf_refs, vt_ref = refs[n_sides + 1:2 * n_sides + 1], refs[2 * n_sides + 1]
    n_tiles = seq // tile
    key_i = lax.broadcasted_iota(jnp.int32, (tile, tile), 0)
    qry_i = lax.broadcasted_iota(jnp.int32, (tile, tile), 1)
    causal = key_i < qry_i
    this_or_later = (qry_i >= key_i).astype(BF16)

    def head_cols(hd):
        return slice(hd * HEAD_DIM, (hd + 1) * HEAD_DIM)

    blocks = [(hd, i, j) for hd in range(heads)
              for i in range(n_tiles - 1, -1, -1) for j in range(i, -1, -1)]
    raw, logit, soft_bf, later_sum, a_bf, acc, carry = {}, {}, {}, {}, {}, {}, {}

    def logit_matmul(nb):
        hd, i, j = blocks[nb]
        raw[nb] = lax.dot_general(k_ref[j * tile:(j + 1) * tile, head_cols(hd)],
                                  q_ref[i * tile:(i + 1) * tile, head_cols(hd)],
                                  _NT, preferred_element_type=F32)

    def softplus(nb):
        _, i, j = blocks[nb]
        z = raw.pop(nb)
        zb = z.astype(BF16)
        ln_term = jnp.log(1.0 + jnp.exp2(-jnp.abs(zb)))
        t = jnp.maximum(zb, 0.0) + (ln_term * _LOG2E_HI + ln_term * _LOG2E_LO)
        if j == i:
            t = jnp.where(causal, t, 0.0)
        logit[nb] = z
        soft_bf[nb] = t.astype(BF16)

    def cumsum_matmul(nb):
        later_sum[nb] = jnp.dot(this_or_later, soft_bf.pop(nb), preferred_element_type=F32)

    def weights(nb):
        hd, i, j = blocks[nb]
        inc = later_sum.pop(nb)
        x = logit.pop(nb) - inc
        if j != i:
            x = x - carry[hd, i]
        a = jnp.exp2(x)
        if j == i:
            a = jnp.where(causal, a, 0.0)
        total = inc[0:1, :]
        carry[hd, i] = total if j == i else carry[hd, i] + total
        a_bf[nb] = a.astype(BF16)

    def value_matmul(nb):
        hd, i, j = blocks[nb]
        d = jnp.dot(vt_ref[hd, :, j * tile:(j + 1) * tile], a_bf.pop(nb), preferred_element_type=F32)
        acc[hd, i] = d if j == i else acc[hd, i] + d
        if j == 0:
            o_ref[i * tile:(i + 1) * tile, head_cols(hd)] = acc.pop((hd, i)).T.astype(o_ref.dtype)

    def transpose_values(hd):
        vt_ref[hd] = v_ref[:, head_cols(hd)].astype(F32).T.astype(BF16)

    def cast_side():
        for src, dst in zip(side_refs, side_bf_refs):
            dst[...] = src[...].astype(BF16)

    l_soft, l_cum, l_w, l_pv = _SB_LAGS
    per_head = len(blocks) // heads
    events = [(2, 1, cast_side)]
    events += [(hd * per_head + 1, 1, functools.partial(transpose_values, hd)) for hd in range(heads)]
    for nb in range(len(blocks)):
        events += [
            (nb, 0, functools.partial(logit_matmul, nb)),
            (nb + l_soft + l_cum + l_w + l_pv, 3, functools.partial(value_matmul, nb)),
            (nb + l_soft + l_cum, 2, functools.partial(cumsum_matmul, nb)),
            (nb + l_soft + l_cum + l_w, 4, functools.partial(weights, nb)),
            (nb + l_soft, 5, functools.partial(softplus, nb)),
        ]
    _emit_in_order(events)


def _sb_attn(qkv, sides, *, batch, seq, tile, heads_per_step):
    kern = functools.partial(_sb_kernel, seq=seq, tile=tile, heads=heads_per_step, n_sides=len(sides))
    groups = N_HEADS_SB // heads_per_step
    base = 3 * (N_HEADS_MOBA // heads_per_step)
    hq, hk, hv = base, base + groups, base + 2 * groups
    blk = (seq, heads_per_step * HEAD_DIM)
    side_specs = [_side_cast_specs(s, batch * groups, lambda b, h: (b * groups + h, 0)) for s in sides]
    return pl.pallas_call(
        kern,
        out_shape=(jax.ShapeDtypeStruct((batch * seq, N_HEADS_SB * HEAD_DIM), ATTN_OUT_DTYPE),
                   *[s[2] for s in side_specs]),
        grid=(batch, groups),
        in_specs=[
            pl.BlockSpec(blk, lambda b, h: (b, hq + h)),
            pl.BlockSpec(blk, lambda b, h: (b, hk + h)),
            pl.BlockSpec(blk, lambda b, h: (b, hv + h)),
            *[s[0] for s in side_specs],
        ],
        out_specs=(pl.BlockSpec(blk, lambda b, h: (b, h)), *[s[1] for s in side_specs]),
        scratch_shapes=[pltpu.VMEM((heads_per_step, HEAD_DIM, seq), BF16)],
        compiler_params=pltpu.CompilerParams(
            dimension_semantics=("arbitrary", "arbitrary"),
            vmem_limit_bytes=VMEM_LIMIT_BYTES),
        name="sb_attn",
    )(qkv, qkv, qkv, *sides)


def _out_proj_kernel(oa_ref, ob_ref, ga_ref, gb_ref, w_ref, x_ref, y_ref):
    d_a = oa_ref.shape[1]
    oa = oa_ref[...].astype(F32)
    ob = ob_ref[...].astype(F32)
    na = ((oa * _rms_scale(oa)) * ga_ref[...]).astype(BF16)
    nb = ((ob * _rms_scale(ob)) * gb_ref[...]).astype(BF16)
    cw = y_ref.shape[1] // EPILOGUE_CHUNKS
    for c in range(EPILOGUE_CHUNKS):
        cols = slice(c * cw, (c + 1) * cw)
        y = (jnp.dot(na, w_ref[0:d_a, cols], preferred_element_type=F32)
             + jnp.dot(nb, w_ref[d_a:, cols], preferred_element_type=F32))
        y_ref[:, cols] = x_ref[:, cols] + y


def _out_proj(o_a, o_b, g_a, g_b, w_bf16, x2, *, tm):
    m, d_a = o_a.shape
    d_b = o_b.shape[1]
    d = w_bf16.shape[1]
    return pl.pallas_call(
        _out_proj_kernel,
        out_shape=jax.ShapeDtypeStruct((m, d), F32),
        grid=(m // tm,),
        in_specs=[
            pl.BlockSpec((tm, d_a), lambda i: (i, 0)),
            pl.BlockSpec((tm, d_b), lambda i: (i, 0)),
            pl.BlockSpec((1, d_a), lambda i: (0, 0)),
            pl.BlockSpec((1, d_b), lambda i: (0, 0)),
            pl.BlockSpec((d_a + d_b, d), lambda i: (0, 0), pipeline_mode=pl.Buffered(1)),
            pl.BlockSpec((tm, d), lambda i: (i, 0)),
        ],
        out_specs=pl.BlockSpec((tm, d), lambda i: (i, 0)),
        compiler_params=pltpu.CompilerParams(
            dimension_semantics=("arbitrary",),
            vmem_limit_bytes=LARGE_VMEM_LIMIT_BYTES),
        name="out_proj",
    )(o_a, o_b, g_a, g_b, w_bf16, x2)


def _mlp_kernel(x_ref, g_ref, wu_ref, wd_ref, gf_ref, o_ref, h_ref, *, n_chunk):
    f = pl.program_id(1)
    n_f = pl.num_programs(1)

    cw = o_ref.shape[1] // n_chunk

    def mlp_slice(h, base):
        u = jnp.dot(h, wu_ref[...], preferred_element_type=F32)
        r = jnp.maximum(u, 0.0)
        act = (r * r).astype(BF16)
        for c in range(n_chunk):
            cols = slice(c * cw, (c + 1) * cw)
            o_ref[:, cols] = base(cols) + jnp.dot(act, wd_ref[:, cols], preferred_element_type=F32)

    @pl.when(f == 0)
    def _():
        x = x_ref[...]
        h = ((x * _rms_scale(x)) * g_ref[...]).astype(BF16)
        h_ref[...] = h
        mlp_slice(h, lambda cols: x_ref[:, cols])

    @pl.when((f > 0) & (f < n_f - 1))
    def _():
        mlp_slice(h_ref[...], lambda cols: o_ref[:, cols])

    @pl.when(f == n_f - 1)
    def _():
        u = jnp.dot(h_ref[...], wu_ref[...], preferred_element_type=F32)
        r = jnp.maximum(u, 0.0)
        act = (r * r).astype(BF16)
        rh = o_ref.shape[0] // n_chunk
        for rc in range(n_chunk):
            rows = slice(rc * rh, (rc + 1) * rh)
            y = o_ref[rows, :] + jnp.dot(act[rows, :], wd_ref[...], preferred_element_type=F32)
            o_ref[rows, :] = (y * _rms_scale(y)) * gf_ref[...]


def _mlp(x1, g, wu_bf16, wd_bf16, g_final, *, tm, tf):
    m, d = x1.shape
    d_ff = wu_bf16.shape[1]
    assert d_ff // tf >= 2, "first and last d_ff slices take different code paths"
    kern = functools.partial(_mlp_kernel, n_chunk=EPILOGUE_CHUNKS)
    return pl.pallas_call(
        kern,
        out_shape=jax.ShapeDtypeStruct((m, d), F32),
        grid=(m // tm, d_ff // tf),
        in_specs=[
            pl.BlockSpec((tm, d), lambda i, f: (i, 0)),
            pl.BlockSpec((1, d), lambda i, f: (0, 0)),
            pl.BlockSpec((d, tf), lambda i, f: (0, f)),
            pl.BlockSpec((tf, d), lambda i, f: (f, 0)),
            pl.BlockSpec((1, d), lambda i, f: (0, 0)),
        ],
        out_specs=pl.BlockSpec((tm, d), lambda i, f: (i, 0)),
        scratch_shapes=[pltpu.VMEM((tm, d), BF16)],
        compiler_params=pltpu.CompilerParams(
            dimension_semantics=("arbitrary", "arbitrary"),
            vmem_limit_bytes=LARGE_VMEM_LIMIT_BYTES),
        name="mlp",
    )(x1, g, wu_bf16, wd_bf16, g_final)


def _rope_tables(seq):
    half = ROPE_DIMS // 2
    inv_freq = ROPE_THETA ** (-np.arange(half, dtype=np.float64) / half)
    ang = np.arange(seq, dtype=np.float64)[:, None] * inv_freq[None, :]
    cos, sin = np.cos(ang), np.sin(ang)
    ones = np.ones((seq, HEAD_DIM - ROPE_DIMS))
    zeros_rest = np.zeros((seq, HEAD_DIM - ROPE_DIMS))
    cos_t = np.concatenate([cos, cos, ones], axis=1)
    sin_t = np.concatenate([-sin, sin, zeros_rest], axis=1)
    return jnp.asarray(cos_t, F32), jnp.asarray(sin_t, F32)


def kernel(x, mix_norm_g, w_in, moba_out_g, sb_out_g, w_out, mlp_norm_g, w_up, w_down, final_norm_g):
    batch, seq, d_model = x.shape
    depth = w_in.shape[0]
    cos_t, sin_t = _rope_tables(seq)
    x2 = x.reshape(batch * seq, d_model)
    for l in range(depth):
        qkv = _qkv_proj(x2, mix_norm_g[l][None, :], w_in[l], cos_t, sin_t,
                        seq=seq, tm=QKV_ROW_TILE, tn=QKV_COL_TILE, n_groups=QKV_ROW_GROUPS)
        o_a, w_up_bf, w_out_bf = _moba_attn(qkv, [w_up[l], w_out[l]], batch=batch, seq=seq,
                                            heads_per_step=ATTN_HEADS_PER_STEP)
        o_b, w_down_bf = _sb_attn(qkv, [w_down[l]], batch=batch, seq=seq, tile=SB_TILE,
                                  heads_per_step=ATTN_HEADS_PER_STEP)
        x1 = _out_proj(o_a, o_b, moba_out_g[l][None, :], sb_out_g[l][None, :],
                       w_out_bf, x2, tm=OUT_PROJ_ROW_TILE)
        last = l == depth - 1
        assert last, "kernel fuses the final RMSNorm into the last layer's MLP; DEPTH must be 1"
        x2 = _mlp(x1, mlp_norm_g[l][None, :], w_up_bf, w_down_bf,
                  final_norm_g[None, :], tm=MLP_ROW_TILE, tf=MLP_FF_TILE)
    return x2.reshape(batch, seq, d_model)
```

```python
import functools
import math

import jax
import jax.numpy as jnp
import numpy as np
from jax import lax
from jax.experimental import pallas as pl
from jax.experimental.pallas import tpu as pltpu

HEAD_DIM = 128
N_HEADS_MOBA = 8
N_HEADS_SB = 8
MOBA_BLOCK = 256
MOBA_TOPK = 3
ROPE_THETA = 500000.0
ROPE_DIMS = HEAD_DIM // 4
EPS = 1e-6
NEG = -1e30

F32 = jnp.float32
BF16 = jnp.bfloat16
ATTN_OUT_DTYPE = BF16
ATTN_HEADS_PER_STEP = 2

_NT = (((1,), (1,)), ((), ()))

VMEM_LIMIT_BYTES = 56 * 1024 * 1024
LARGE_VMEM_LIMIT_BYTES = 62 * 1024 * 1024

QKV_ROW_TILE, QKV_COL_TILE, QKV_ROW_GROUPS = 1024, 1024, 2
OUT_PROJ_ROW_TILE = 1024
MLP_ROW_TILE, MLP_FF_TILE = 1024, 1024
SB_TILE = 256
EPILOGUE_CHUNKS = 4

Q_SCALE = HEAD_DIM ** -0.5 * math.log2(math.e)


def _rms_scale(x):
    return lax.rsqrt(jnp.mean(x * x, axis=-1, keepdims=True) + EPS)


def _emit_in_order(events):
    for _, _, thunk in sorted(events, key=lambda e: (e[0], e[1])):
        thunk()


def _qkv_kernel(x_ref, g_ref, w_ref, cos_ref, sin_ref, o_ref, h_ref):
    j = pl.program_id(1)
    i = pl.program_id(2)
    tm = x_ref.shape[0]
    rows = pl.ds(pl.multiple_of(i * tm, tm), tm)

    def project():
        return jnp.dot(h_ref[rows, :], w_ref[...].astype(BF16), preferred_element_type=F32)

    def rope(y, post_scale, r0):
        n_heads = y.shape[1] // HEAD_DIM
        half = ROPE_DIMS // 2
        nr = y.shape[0]
        lane = lax.broadcasted_iota(jnp.int32, (nr, HEAD_DIM), 1)
        partner = jnp.where(lane < ROPE_DIMS, lane ^ half, lane)
        for hd in range(n_heads):
            t = y[:, hd * HEAD_DIM:(hd + 1) * HEAD_DIM]
            swapped = jnp.take_along_axis(t, partner, axis=1)
            r = t * cos_ref[r0:r0 + nr, :] + swapped * sin_ref[r0:r0 + nr, :]
            if post_scale is not None:
                r = r * post_scale
            o_ref[r0:r0 + nr, hd * HEAD_DIM:(hd + 1) * HEAD_DIM] = r.astype(o_ref.dtype)

    rh = tm // EPILOGUE_CHUNKS

    def chunk_rows(rc):
        return pl.ds(pl.multiple_of(i * tm + rc * rh, rh), rh)

    @pl.when(j == 0)
    def _():
        wb = w_ref[...].astype(BF16)
        for rc in range(EPILOGUE_CHUNKS):
            x = x_ref[rc * rh:(rc + 1) * rh, :]
            h = ((x * _rms_scale(x)) * g_ref[...]).astype(BF16)
            h_ref[chunk_rows(rc), :] = h
            rope(jnp.dot(h, wb, preferred_element_type=F32), Q_SCALE, rc * rh)

    @pl.when(j == 1)
    def _():
        wb = w_ref[...].astype(BF16)
        for rc in range(EPILOGUE_CHUNKS):
            rope(jnp.dot(h_ref[chunk_rows(rc), :], wb, preferred_element_type=F32), None, rc * rh)

    @pl.when(j == 3)
    def _():
        o_ref[...] = (project() * Q_SCALE).astype(o_ref.dtype)

    @pl.when((j == 2) | (j >= 4))
    def _():
        o_ref[...] = project().astype(o_ref.dtype)


def _side_cast_specs(side, n_steps, index_map):
    rows, cols = side.shape
    slab = rows // n_steps
    assert slab * n_steps == rows and slab % 16 == 0, (rows, n_steps)
    spec = pl.BlockSpec((slab, cols), index_map)
    return spec, spec, jax.ShapeDtypeStruct((rows, cols), BF16)


def _qkv_proj(x2, g, w, cos_t, sin_t, *, seq, tm, tn, n_groups):
    m, d = x2.shape
    n = w.shape[1]
    assert tn == N_HEADS_MOBA * HEAD_DIM == N_HEADS_SB * HEAD_DIM and n == 6 * tn, (tn, n)
    tiles = m // (tm * n_groups)
    pos_blocks = seq // tm
    n_rope_tiles = 2

    def x_tile(gr, j, i):
        return gr * tiles + jnp.where(j == 0, i, tiles - 1)

    tab_spec = pl.BlockSpec(
        (tm, HEAD_DIM), lambda gr, j, i: (jnp.where(j < n_rope_tiles, (gr * tiles + i) % pos_blocks, 0), 0))
    return pl.pallas_call(
        _qkv_kernel,
        out_shape=jax.ShapeDtypeStruct((m, n), BF16),
        grid=(n_groups, n // tn, tiles),
        in_specs=[
            pl.BlockSpec((tm, d), lambda gr, j, i: (x_tile(gr, j, i), 0)),
            pl.BlockSpec((1, d), lambda gr, j, i: (0, 0)),
            pl.BlockSpec((d, tn), lambda gr, j, i: (0, j)),
            tab_spec, tab_spec,
        ],
        out_specs=pl.BlockSpec((tm, tn), lambda gr, j, i: (gr * tiles + i, j)),
        scratch_shapes=[pltpu.VMEM((tiles * tm, d), BF16)],
        compiler_params=pltpu.CompilerParams(
            dimension_semantics=("arbitrary", "arbitrary", "arbitrary"),
            vmem_limit_bytes=LARGE_VMEM_LIMIT_BYTES),
        name="qkv_proj",
    )(x2, g, w, cos_t, sin_t)


_MOBA_LAG_MASK, _MOBA_LAG_EXP, _MOBA_LAG_PV = 4, 3, 2
_MOBA_SUM_ROWS = 16
_MOBA_SCORE_RING = 16


def _moba_kernel(q_ref, k_ref, v_ref, *refs, seq, heads, n_sides):
    side_refs, o_ref = refs[:n_sides], refs[n_sides]
    side_bf_refs, vt_ref, score_ref = refs[n_sides + 1:2 * n_sides + 1], refs[2 * n_sides + 1], refs[2 * n_sides + 2]
    blk = MOBA_BLOCK
    n_blk = seq // blk

    key_i = lax.broadcasted_iota(jnp.int32, (blk, blk), 0)
    qry_i = lax.broadcasted_iota(jnp.int32, (blk, blk), 1)
    causal = key_i <= qry_i
    blk_id = lax.broadcasted_iota(jnp.int32, (n_blk, blk), 0)

    def head_cols(hd):
        return slice(hd * HEAD_DIM, (hd + 1) * HEAD_DIM)

    blocks = [(hd, i, j) for hd in range(heads)
              for i in range(n_blk - 1, -1, -1) for j in ([i] + list(range(i)))]
    last_of_tile = {i: (i - 1 if i else 0) for i in range(n_blk)}
    km_rows, raw, gate_raw, bias, col_max, p_bf, acc = {}, {}, {}, {}, {}, {}, {}

    def mean_keys(hd):
        kf = k_ref[:, head_cols(hd)].astype(F32)
        k_mean = jnp.concatenate(
            [jnp.mean(kf[n * blk:(n + 1) * blk, :], axis=0, keepdims=True) for n in range(n_blk)], axis=0)
        km_hi = k_mean.astype(BF16)
        km_lo = (k_mean - km_hi.astype(F32)).astype(BF16)
        km_rows[hd] = jnp.concatenate([km_hi, km_lo], axis=0)

    def transpose_values(hd):
        vt_ref[hd, 0:HEAD_DIM, :] = v_ref[:, head_cols(hd)].astype(F32).T.astype(BF16)
        extra = lax.broadcasted_iota(jnp.int32, (_MOBA_SUM_ROWS, seq), 0)
        vt_ref[hd, HEAD_DIM:, :] = (extra == 0).astype(BF16)

    def score_matmul(nb):
        hd, i, j = blocks[nb]
        q_i = q_ref[i * blk:(i + 1) * blk, head_cols(hd)]
        k_j = k_ref[j * blk:(j + 1) * blk, head_cols(hd)]
        if j == i:
            r = lax.dot_general(jnp.concatenate([k_j, km_rows[hd]], axis=0), q_i, _NT,
                                preferred_element_type=F32)
            raw[nb] = r[0:blk, :]
            gate_raw[hd, i] = r[blk:blk + n_blk, :] + r[blk + n_blk:blk + 2 * n_blk, :]
        else:
            raw[nb] = lax.dot_general(k_j, q_i, _NT, preferred_element_type=F32)

    def select_blocks(hd, i):
        past = blk_id < i
        g = jnp.where(past, gate_raw.pop((hd, i)), NEG)
        rank = jnp.zeros((n_blk, blk), jnp.int32)
        for other in range(n_blk):
            g_o = g[other:other + 1, :]
            beats = (g_o > g) | ((g_o == g) & (other < blk_id))
            rank = rank + beats.astype(jnp.int32)
        bias[hd, i] = jnp.where(past & (rank < MOBA_TOPK), 0.0, NEG).astype(F32)

    def mask_and_max(nb):
        hd, i, j = blocks[nb]
        s = raw.pop(nb)
        if j == i:
            s = jnp.where(causal, s, NEG)
            select_blocks(hd, i)
        else:
            s = s + bias[hd, i][j:j + 1, :]
        score_ref[nb % _MOBA_SCORE_RING] = s
        cm = jnp.max(s, axis=0, keepdims=True)
        col_max[hd, i] = cm if j == i else jnp.maximum(col_max[hd, i], cm)

    def exponentiate(nb):
        hd, i, j = blocks[nb]
        p = jnp.exp2(score_ref[nb % _MOBA_SCORE_RING] - col_max[hd, i])
        p_bf[nb] = p.astype(BF16)

    def value_matmul(nb):
        hd, i, j = blocks[nb]
        d = jnp.dot(vt_ref[hd, :, j * blk:(j + 1) * blk], p_bf.pop(nb), preferred_element_type=F32)
        acc[hd, i] = d if j == i else acc[hd, i] + d
        if j == last_of_tile[i]:
            total = acc.pop((hd, i))
            out = (total[0:HEAD_DIM, :] / total[HEAD_DIM:HEAD_DIM + 1, :]).T
            o_ref[i * blk:(i + 1) * blk, head_cols(hd)] = out.astype(o_ref.dtype)

    def cast_side():
        for src, dst in zip(side_refs, side_bf_refs):
            dst[...] = src[...].astype(BF16)

    mask_step = [nb + _MOBA_LAG_MASK for nb in range(len(blocks))]
    tile_done = {}
    for nb, (hd, i, _) in enumerate(blocks):
        tile_done[hd, i] = max(tile_done.get((hd, i), 0), mask_step[nb])
    per_head = len(blocks) // heads
    events = [(2, 1, cast_side)]
    for hd in range(heads):
        first = hd * per_head
        events += [(max(first - _MOBA_LAG_MASK, 0), -1, functools.partial(mean_keys, hd)),
                   (first + 1, 1, functools.partial(transpose_values, hd))]
    exp_step = -1
    for nb, (hd, i, _) in enumerate(blocks):
        exp_step = max(exp_step + 1, tile_done[hd, i] + _MOBA_LAG_EXP)
        reuse = nb + _MOBA_SCORE_RING
        assert reuse >= len(blocks) or mask_step[reuse] >= exp_step, "score ring too small"
        events += [
            (nb, 0, functools.partial(score_matmul, nb)),
            (exp_step + _MOBA_LAG_PV, 2, functools.partial(value_matmul, nb)),
            (exp_step, 3, functools.partial(exponentiate, nb)),
            (mask_step[nb], 4, functools.partial(mask_and_max, nb)),
        ]
    _emit_in_order(events)


def _moba_attn(qkv, sides, *, batch, seq, heads_per_step):
    kern = functools.partial(_moba_kernel, seq=seq, heads=heads_per_step, n_sides=len(sides))
    groups = N_HEADS_MOBA // heads_per_step
    hq, hk, hv = 0, groups, 2 * groups
    blk = (seq, heads_per_step * HEAD_DIM)
    side_specs = [_side_cast_specs(s, batch * groups, lambda b, h: (b * groups + h, 0)) for s in sides]
    return pl.pallas_call(
        kern,
        out_shape=(jax.ShapeDtypeStruct((batch * seq, N_HEADS_MOBA * HEAD_DIM), ATTN_OUT_DTYPE),
                   *[s[2] for s in side_specs]),
        grid=(batch, groups),
        in_specs=[
            pl.BlockSpec(blk, lambda b, h: (b, hq + h)),
            pl.BlockSpec(blk, lambda b, h: (b, hk + h)),
            pl.BlockSpec(blk, lambda b, h: (b, hv + h)),
            *[s[0] for s in side_specs],
        ],
        out_specs=(pl.BlockSpec(blk, lambda b, h: (b, h)), *[s[1] for s in side_specs]),
        scratch_shapes=[pltpu.VMEM((heads_per_step, HEAD_DIM + _MOBA_SUM_ROWS, seq), BF16),
                        pltpu.VMEM((_MOBA_SCORE_RING, MOBA_BLOCK, MOBA_BLOCK), F32)],
        compiler_params=pltpu.CompilerParams(
            dimension_semantics=("arbitrary", "arbitrary"),
            vmem_limit_bytes=VMEM_LIMIT_BYTES),
        name="moba_attn",
    )(qkv, qkv, qkv, *sides)


_SB_LAGS = (1, 1, 2, 1)
_LOG2E_HI = float(np.asarray(math.log2(math.e), dtype=jnp.bfloat16))
_LOG2E_LO = float(np.asarray(math.log2(math.e) - _LOG2E_HI, dtype=jnp.bfloat16))


def _sb_kernel(q_ref, k_ref, v_ref, *refs, seq, tile, heads, n_sides):
    side_refs, o_ref = refs[:n_sides], refs[n_sides]
    side_bf_refs, vt_ref = refs[n_sides + 1:2 * n_sides + 1], refs[2 * n_sides + 1]
    n_tiles = seq // tile
    key_i = lax.broadcasted_iota(jnp.int32, (tile, tile), 0)
    qry_i = lax.broadcasted_iota(jnp.int32, (tile, tile), 1)
    causal = key_i < qry_i
    this_or_later = (qry_i >= key_i).astype(BF16)

    def head_cols(hd):
        return slice(hd * HEAD_DIM, (hd + 1) * HEAD_DIM)

    blocks = [(hd, i, j) for hd in range(heads)
              for i in range(n_tiles - 1, -1, -1) for j in range(i, -1, -1)]
    raw, logit, soft_bf, later_sum, a_bf, acc, carry = {}, {}, {}, {}, {}, {}, {}

    def logit_matmul(nb):
        hd, i, j = blocks[nb]
        raw[nb] = lax.dot_general(k_ref[j * tile:(j + 1) * tile, head_cols(hd)],
                                  q_ref[i * tile:(i + 1) * tile, head_cols(hd)],
                                  _NT, preferred_element_type=F32)

    def softplus(nb):
        _, i, j = blocks[nb]
        z = raw.pop(nb)
        zb = z.astype(BF16)
        ln_term = jnp.log(1.0 + jnp.exp2(-jnp.abs(zb)))
        t = jnp.maximum(zb, 0.0) + (ln_term * _LOG2E_HI + ln_term * _LOG2E_LO)
        if j == i:
            t = jnp.where(causal, t, 0.0)
        logit[nb] = z
        soft_bf[nb] = t.astype(BF16)

    def cumsum_matmul(nb):
        later_sum[nb] = jnp.dot(this_or_later, soft_bf.pop(nb), preferred_element_type=F32)

    def weights(nb):
        hd, i, j = blocks[nb]
        inc = later_sum.pop(nb)
        x = logit.pop(nb) - inc
        if j != i:
            x = x - carry[hd, i]
        a = jnp.exp2(x)
        if j == i:
            a = jnp.where(causal, a, 0.0)
        total = inc[0:1, :]
        carry[hd, i] = total if j == i else carry[hd, i] + total
        a_bf[nb] = a.astype(BF16)

    def value_matmul(nb):
        hd, i, j = blocks[nb]
        d = jnp.dot(vt_ref[hd, :, j * tile:(j + 1) * tile], a_bf.pop(nb), preferred_element_type=F32)
        acc[hd, i] = d if j == i else acc[hd, i] + d
        if j == 0:
            o_ref[i * tile:(i + 1) * tile, head_cols(hd)] = acc.pop((hd, i)).T.astype(o_ref.dtype)

    def transpose_values(hd):
        vt_ref[hd] = v_ref[:, head_cols(hd)].astype(F32).T.astype(BF16)

    def cast_side():
        for src, dst in zip(side_refs, side_bf_refs):
            dst[...] = src[...].astype(BF16)

    l_soft, l_cum, l_w, l_pv = _SB_LAGS
    per_head = len(blocks) // heads
    events = [(2, 1, cast_side)]
    events += [(hd * per_head + 1, 1, functools.partial(transpose_values, hd)) for hd in range(heads)]
    for nb in range(len(blocks)):
        events += [
            (nb, 0, functools.partial(logit_matmul, nb)),
            (nb + l_soft + l_cum + l_w + l_pv, 3, functools.partial(value_matmul, nb)),
            (nb + l_soft + l_cum, 2, functools.partial(cumsum_matmul, nb)),
            (nb + l_soft + l_cum + l_w, 4, functools.partial(weights, nb)),
            (nb + l_soft, 5, functools.partial(softplus, nb)),
        ]
    _emit_in_order(events)


def _sb_attn(qkv, sides, *, batch, seq, tile, heads_per_step):
    kern = functools.partial(_sb_kernel, seq=seq, tile=tile, heads=heads_per_step, n_sides=len(sides))
    groups = N_HEADS_SB // heads_per_step
    base = 3 * (N_HEADS_MOBA // heads_per_step)
    hq, hk, hv = base, base + groups, base + 2 * groups
    blk = (seq, heads_per_step * HEAD_DIM)
    side_specs = [_side_cast_specs(s, batch * groups, lambda b, h: (b * groups + h, 0)) for s in sides]
    return pl.pallas_call(
        kern,
        out_shape=(jax.ShapeDtypeStruct((batch * seq, N_HEADS_SB * HEAD_DIM), ATTN_OUT_DTYPE),
                   *[s[2] for s in side_specs]),
        grid=(batch, groups),
        in_specs=[
            pl.BlockSpec(blk, lambda b, h: (b, hq + h)),
            pl.BlockSpec(blk, lambda b, h: (b, hk + h)),
            pl.BlockSpec(blk, lambda b, h: (b, hv + h)),
            *[s[0] for s in side_specs],
        ],
        out_specs=(pl.BlockSpec(blk, lambda b, h: (b, h)), *[s[1] for s in side_specs]),
        scratch_shapes=[pltpu.VMEM((heads_per_step, HEAD_DIM, seq), BF16)],
        compiler_params=pltpu.CompilerParams(
            dimension_semantics=("arbitrary", "arbitrary"),
            vmem_limit_bytes=VMEM_LIMIT_BYTES),
        name="sb_attn",
    )(qkv, qkv, qkv, *sides)


def _out_proj_kernel(oa_ref, ob_ref, ga_ref, gb_ref, w_ref, x_ref, y_ref):
    d_a = oa_ref.shape[1]
    oa = oa_ref[...].astype(F32)
    ob = ob_ref[...].astype(F32)
    na = ((oa * _rms_scale(oa)) * ga_ref[...]).astype(BF16)
    nb = ((ob * _rms_scale(ob)) * gb_ref[...]).astype(BF16)
    cw = y_ref.shape[1] // EPILOGUE_CHUNKS
    for c in range(EPILOGUE_CHUNKS):
        cols = slice(c * cw, (c + 1) * cw)
        y = (jnp.dot(na, w_ref[0:d_a, cols], preferred_element_type=F32)
             + jnp.dot(nb, w_ref[d_a:, cols], preferred_element_type=F32))
        y_ref[:, cols] = x_ref[:, cols] + y


def _out_proj(o_a, o_b, g_a, g_b, w_bf16, x2, *, tm):
    m, d_a = o_a.shape
    d_b = o_b.shape[1]
    d = w_bf16.shape[1]
    return pl.pallas_call(
        _out_proj_kernel,
        out_shape=jax.ShapeDtypeStruct((m, d), F32),
        grid=(m // tm,),
        in_specs=[
            pl.BlockSpec((tm, d_a), lambda i: (i, 0)),
            pl.BlockSpec((tm, d_b), lambda i: (i, 0)),
            pl.BlockSpec((1, d_a), lambda i: (0, 0)),
            pl.BlockSpec((1, d_b), lambda i: (0, 0)),
            pl.BlockSpec((d_a + d_b, d), lambda i: (0, 0), pipeline_mode=pl.Buffered(1)),
            pl.BlockSpec((tm, d), lambda i: (i, 0)),
        ],
        out_specs=pl.BlockSpec((tm, d), lambda i: (i, 0)),
        compiler_params=pltpu.CompilerParams(
            dimension_semantics=("arbitrary",),
            vmem_limit_bytes=LARGE_VMEM_LIMIT_BYTES),
        name="out_proj",
    )(o_a, o_b, g_a, g_b, w_bf16, x2)


def _mlp_kernel(x_ref, g_ref, wu_ref, wd_ref, gf_ref, o_ref, h_ref, *, n_chunk):
    f = pl.program_id(1)
    n_f = pl.num_programs(1)

    cw = o_ref.shape[1] // n_chunk

    def mlp_slice(h, base):
        u = jnp.dot(h, wu_ref[...], preferred_element_type=F32)
        r = jnp.maximum(u, 0.0)
        act = (r * r).astype(BF16)
        for c in range(n_chunk):
            cols = slice(c * cw, (c + 1) * cw)
            o_ref[:, cols] = base(cols) + jnp.dot(act, wd_ref[:, cols], preferred_element_type=F32)

    @pl.when(f == 0)
    def _():
        x = x_ref[...]
        h = ((x * _rms_scale(x)) * g_ref[...]).astype(BF16)
        h_ref[...] = h
        mlp_slice(h, lambda cols: x_ref[:, cols])

    @pl.when((f > 0) & (f < n_f - 1))
    def _():
        mlp_slice(h_ref[...], lambda cols: o_ref[:, cols])

    @pl.when(f == n_f - 1)
    def _():
        u = jnp.dot(h_ref[...], wu_ref[...], preferred_element_type=F32)
        r = jnp.maximum(u, 0.0)
        act = (r * r).astype(BF16)
        rh = o_ref.shape[0] // n_chunk
        for rc in range(n_chunk):
            rows = slice(rc * rh, (rc + 1) * rh)
            y = o_ref[rows, :] + jnp.dot(act[rows, :], wd_ref[...], preferred_element_type=F32)
            o_ref[rows, :] = (y * _rms_scale(y)) * gf_ref[...]


def _mlp(x1, g, wu_bf16, wd_bf16, g_final, *, tm, tf):
    m, d = x1.shape
    d_ff = wu_bf16.shape[1]
    assert d_ff // tf >= 2, "first and last d_ff slices take different code paths"
    kern = functools.partial(_mlp_kernel, n_chunk=EPILOGUE_CHUNKS)
    return pl.pallas_call(
        kern,
        out_shape=jax.ShapeDtypeStruct((m, d), F32),
        grid=(m // tm, d_ff // tf),
        in_specs=[
            pl.BlockSpec((tm, d), lambda i, f: (i, 0)),
            pl.BlockSpec((1, d), lambda i, f: (0, 0)),
            pl.BlockSpec((d, tf), lambda i, f: (0, f)),
            pl.BlockSpec((tf, d), lambda i, f: (f, 0)),
            pl.BlockSpec((1, d), lambda i, f: (0, 0)),
        ],
        out_specs=pl.BlockSpec((tm, d), lambda i, f: (i, 0)),
        scratch_shapes=[pltpu.VMEM((tm, d), BF16)],
        compiler_params=pltpu.CompilerParams(
            dimension_semantics=("arbitrary", "arbitrary"),
            vmem_limit_bytes=LARGE_VMEM_LIMIT_BYTES),
        name="mlp",
    )(x1, g, wu_bf16, wd_bf16, g_final)


def _rope_tables(seq):
    half = ROPE_DIMS // 2
    inv_freq = ROPE_THETA ** (-np.arange(half, dtype=np.float64) / half)
    ang = np.arange(seq, dtype=np.float64)[:, None] * inv_freq[None, :]
    cos, sin = np.cos(ang), np.sin(ang)
    ones = np.ones((seq, HEAD_DIM - ROPE_DIMS))
    zeros_rest = np.zeros((seq, HEAD_DIM - ROPE_DIMS))
    cos_t = np.concatenate([cos, cos, ones], axis=1)
    sin_t = np.concatenate([-sin, sin, zeros_rest], axis=1)
    return jnp.asarray(cos_t, F32), jnp.asarray(sin_t, F32)


def kernel(x, mix_norm_g, w_in, moba_out_g, sb_out_g, w_out, mlp_norm_g, w_up, w_down, final_norm_g):
    batch, seq, d_model = x.shape
    depth = w_in.shape[0]
    cos_t, sin_t = _rope_tables(seq)
    x2 = x.reshape(batch * seq, d_model)
    for l in range(depth):
        qkv = _qkv_proj(x2, mix_norm_g[l][None, :], w_in[l], cos_t, sin_t,
                        seq=seq, tm=QKV_ROW_TILE, tn=QKV_COL_TILE, n_groups=QKV_ROW_GROUPS)
        o_a, w_up_bf, w_out_bf = _moba_attn(qkv, [w_up[l], w_out[l]], batch=batch, seq=seq,
                                            heads_per_step=ATTN_HEADS_PER_STEP)
        o_b, w_down_bf = _sb_attn(qkv, [w_down[l]], batch=batch, seq=seq, tile=SB_TILE,
                                  heads_per_step=ATTN_HEADS_PER_STEP)
        x1 = _out_proj(o_a, o_b, moba_out_g[l][None, :], sb_out_g[l][None, :],
                       w_out_bf, x2, tm=OUT_PROJ_ROW_TILE)
        last = l == depth - 1
        assert last, "kernel fuses the final RMSNorm into the last layer's MLP; DEPTH must be 1"
        x2 = _mlp(x1, mlp_norm_g[l][None, :], w_up_bf, w_down_bf,
                  final_norm_g[None, :], tm=MLP_ROW_TILE, tf=MLP_FF_TILE)
    return x2.reshape(batch, seq, d_model)
```
